```python
import math
import jax
import jax.numpy as jnp
from jax import lax
import numpy as np

D_MODEL = 1024
BATCH = 16
SEQ = 2048
DEPTH = 4

CHUNK = 64
Q_BLOCK = 128
HEAD_DIM = 64
CONV_CH = D_MODEL // 2
CONV_K = 3
FOX_HEADS = (D_MODEL // 2) // HEAD_DIM
FOX_WIDTH = FOX_HEADS * HEAD_DIM
DIFF_HEADS = D_MODEL // (2 * HEAD_DIM)
N_GROUPS = 4
EXPERTS_PER_GROUP = 8
N_EXPERTS = N_GROUPS * EXPERTS_PER_GROUP
TOP_K = 2
D_EXPERT = D_MODEL // 2
MOE_BLOCK = 128
N_EVEN = (DEPTH + 1) // 2
N_ODD = DEPTH // 2
EVEN_IN = 3 * CONV_CH + 3 * FOX_WIDTH + FOX_HEADS
ODD_IN = 3 * D_MODEL
ALPHA = (2.0 * DEPTH) ** 0.25
BETA = (8.0 * DEPTH) ** -0.25
LN_EPS = 1e-5
RMS_EPS = 1e-5
FORGET_BIAS = 3.0
NEG_INF = -1e30

kernel_name = 'hybrid_chunk_stream_block'


def layer_norm(x, g, b):
    xf = x.astype(jnp.float32)
    mu = jnp.mean(xf, axis=-1, keepdims=True)
    var = jnp.mean(jnp.square(xf - mu), axis=-1, keepdims=True)
    return ((xf - mu) * lax.rsqrt(var + LN_EPS) * g + b).astype(x.dtype)


def short_gated_conv(xv, gate_b, gate_c, conv_w):
    u = gate_c * xv
    seq = u.shape[1]
    up = jnp.pad(u, ((0, 0), (CONV_K - 1, 0), (0, 0)))
    y = up[:, 0:seq] * conv_w[:, 0]
    for j in range(1, CONV_K):
        y = y + up[:, j:j + seq] * conv_w[:, j]
    return gate_b * y


def forgetting_attention(q, k, v, f_logit):
    seq = q.shape[1]
    scale = HEAD_DIM ** -0.5
    c = jnp.cumsum(jax.nn.log_sigmoid(f_logit.astype(jnp.float32)), axis=1)
    c = jnp.transpose(c, (0, 2, 1))
    outs = []
    for blk in range(seq // Q_BLOCK):
        q_lo, q_hi = blk * Q_BLOCK, (blk + 1) * Q_BLOCK
        t = jnp.arange(q_lo, q_hi)[:, None]
        s = jnp.arange(q_hi)[None, :]
        logits = jnp.einsum('bqhd,bkhd->bhqk', q[:, q_lo:q_hi], k[:, :q_hi],
                            preferred_element_type=jnp.float32) * scale
        logits = logits + c[:, :, q_lo:q_hi, None] - c[:, :, None, :q_hi]
        p = jax.nn.softmax(jnp.where(s <= t, logits, NEG_INF), axis=-1)
        outs.append(jnp.einsum('bhqk,bkhd->bqhd', p.astype(v.dtype), v[:, :q_hi]))
    return jnp.concatenate(outs, axis=1)


def alibi_slopes(n_heads):
    return jnp.asarray(2.0 ** (-8.0 * np.arange(1, n_heads + 1) / n_heads), dtype=jnp.float32)


def chunk_block_probs(q, k, bias, visible, q_lo, q_hi):
    scale = HEAD_DIM ** -0.5
    logits = jnp.einsum('bqhd,bkhd->bhqk', q[:, q_lo:q_hi], k[:, :q_hi],
                        preferred_element_type=jnp.float32) * scale + bias
    return jax.nn.softmax(jnp.where(visible, logits, NEG_INF), axis=-1)


def differential_attention(q1, q2, k1, k2, v, lam, lam_init, subln_g):
    seq = q1.shape[1]
    slopes = alibi_slopes(DIFF_HEADS)[:, None, None]
    outs = []
    for blk in range(seq // Q_BLOCK):
        q_lo, q_hi = blk * Q_BLOCK, (blk + 1) * Q_BLOCK
        t = jnp.arange(q_lo, q_hi)[:, None]
        s = jnp.arange(q_hi)[None, :]
        bias = -slopes * jnp.abs(t - s).astype(jnp.float32)
        visible = (s // CHUNK) <= (t // CHUNK)
        p = (chunk_block_probs(q1, k1, bias, visible, q_lo, q_hi)
             - lam * chunk_block_probs(q2, k2, bias, visible, q_lo, q_hi))
        outs.append(jnp.einsum('bhqk,bkhe->bqhe', p.astype(v.dtype), v[:, :q_hi]))
    o = jnp.concatenate(outs, axis=1).astype(jnp.float32)
    o = o * lax.rsqrt(jnp.mean(jnp.square(o), axis=-1, keepdims=True) + RMS_EPS)
    return (o * subln_g * (1.0 - lam_init)).astype(v.dtype)


def hierarchical_moe(x, w_group, b_group, w_expert, b_expert, w_gate, w_up, w_down):
    bsz, seq, dm = x.shape
    n_tok = bsz * seq
    xf = x.reshape(n_tok, dm)
    g_logit = (xf @ w_group + b_group).astype(jnp.float32)
    g_prob = jax.nn.softmax(g_logit, axis=-1)
    g_sel = jnp.argmax(g_logit, axis=-1)
    g_w = jnp.take_along_axis(g_prob, g_sel[:, None], axis=-1)[:, 0]
    e_logit = (xf @ w_expert + b_expert).astype(jnp.float32).reshape(n_tok, N_GROUPS, EXPERTS_PER_GROUP)
    e_in = jnp.take_along_axis(e_logit, g_sel[:, None, None], axis=1)[:, 0]
    top_v, top_i = lax.top_k(e_in, TOP_K)
    top_w = jax.nn.softmax(top_v, axis=-1) * g_w[:, None]
    eid = (g_sel[:, None] * EXPERTS_PER_GROUP + top_i).reshape(-1)
    tok = jnp.repeat(jnp.arange(n_tok), TOP_K)
    wt = top_w.reshape(-1)
    order = jnp.argsort(eid)
    s_eid, s_tok, s_wt = eid[order], tok[order], wt[order]
    counts = jnp.bincount(eid, length=N_EXPERTS)
    padded = (counts + MOE_BLOCK - 1) // MOE_BLOCK * MOE_BLOCK
    pad_end = jnp.cumsum(padded)
    pad_start = pad_end - padded
    start = jnp.cumsum(counts) - counts
    dest = pad_start[s_eid] + jnp.arange(eid.shape[0]) - start[s_eid]
    cap = -(-(n_tok * TOP_K) // MOE_BLOCK) * MOE_BLOCK + N_EXPERTS * MOE_BLOCK
    n_blocks = cap // MOE_BLOCK
    xb = jnp.zeros((cap, dm), x.dtype).at[dest].set(xf[s_tok])
    blk_e = jnp.minimum(jnp.searchsorted(pad_end, jnp.arange(n_blocks) * MOE_BLOCK, side='right'),
                        N_EXPERTS - 1)

    def expert_block(args):
        xblk, e = args
        hid = jax.nn.silu(xblk @ w_gate[e]) * (xblk @ w_up[e])
        return hid @ w_down[e]

    yb = lax.map(expert_block, (xb.reshape(n_blocks, MOE_BLOCK, dm), blk_e)).reshape(cap, dm)
    y = jnp.zeros((n_tok, dm), x.dtype).at[s_tok].add(yb[dest] * s_wt[:, None].astype(x.dtype))
    return y.reshape(bsz, seq, dm)


def setup_inputs(seed: int = 0) -> dict:
    key = jax.random.key(seed)
    ks = jax.random.split(key, 24)
    nrm = jax.random.normal
    f32 = jnp.float32
    even_scale = np.ones((EVEN_IN,), np.float32)
    even_scale[2 * CONV_CH:3 * CONV_CH] = BETA
    even_scale[3 * CONV_CH + 2 * FOX_WIDTH:3 * CONV_CH + 3 * FOX_WIDTH] = BETA
    odd_scale = np.ones((ODD_IN,), np.float32)
    odd_scale[2 * D_MODEL:] = BETA
    return {
        'x': nrm(ks[0], (BATCH, SEQ, D_MODEL), f32),
        'ab_w_in': nrm(ks[1], (N_EVEN, D_MODEL, EVEN_IN), f32) * D_MODEL ** -0.5 * jnp.asarray(even_scale),
        'ab_b_forget': FORGET_BIAS + 0.5 * nrm(ks[2], (N_EVEN, FOX_HEADS), f32),
        'ab_conv_w': nrm(ks[3], (N_EVEN, CONV_CH, CONV_K), f32) * CONV_K ** -0.5,
        'ab_w_out': nrm(ks[4], (N_EVEN, D_MODEL, D_MODEL), f32) * D_MODEL ** -0.5 * BETA,
        'c_w_in': nrm(ks[5], (N_ODD, D_MODEL, ODD_IN), f32) * D_MODEL ** -0.5 * jnp.asarray(odd_scale),
        'c_lam_q1': 0.1 * nrm(ks[6], (N_ODD, HEAD_DIM), f32),
        'c_lam_k1': 0.1 * nrm(ks[7], (N_ODD, HEAD_DIM), f32),
        'c_lam_q2': 0.1 * nrm(ks[8], (N_ODD, HEAD_DIM), f32),
        'c_lam_k2': 0.1 * nrm(ks[9], (N_ODD, HEAD_DIM), f32),
        'c_subln_g': 1.0 + 0.02 * nrm(ks[10], (N_ODD, 2 * HEAD_DIM), f32),
        'c_w_out': nrm(ks[11], (N_ODD, D_MODEL, D_MODEL), f32) * D_MODEL ** -0.5 * BETA,
        'ln_mix_g': 1.0 + 0.02 * nrm(ks[12], (DEPTH, D_MODEL), f32),
        'ln_mix_b': 0.02 * nrm(ks[13], (DEPTH, D_MODEL), f32),
        'ln_ffn_g': 1.0 + 0.02 * nrm(ks[14], (DEPTH, D_MODEL), f32),
        'ln_ffn_b': 0.02 * nrm(ks[15], (DEPTH, D_MODEL), f32),
        'moe_w_group': nrm(ks[16], (DEPTH, D_MODEL, N_GROUPS), f32) * D_MODEL ** -0.5,
        'moe_b_group': 0.01 * nrm(ks[17], (DEPTH, N_GROUPS), f32),
        'moe_w_expert': nrm(ks[18], (DEPTH, D_MODEL, N_EXPERTS), f32) * D_MODEL ** -0.5,
        'moe_b_expert': 0.01 * nrm(ks[19], (DEPTH, N_EXPERTS), f32),
        'moe_w_gate': nrm(ks[20], (DEPTH, N_EXPERTS, D_MODEL, D_EXPERT), f32) * D_MODEL ** -0.5,
        'moe_w_up': nrm(ks[21], (DEPTH, N_EXPERTS, D_MODEL, D_EXPERT), f32) * D_MODEL ** -0.5,
        'moe_w_down': nrm(ks[22], (DEPTH, N_EXPERTS, D_EXPERT, D_MODEL), f32) * D_EXPERT ** -0.5 * BETA,
    }


def reference(x, ab_w_in, ab_b_forget, ab_conv_w, ab_w_out, c_w_in, c_lam_q1, c_lam_k1,
              c_lam_q2, c_lam_k2, c_subln_g, c_w_out, ln_mix_g, ln_mix_b, ln_ffn_g, ln_ffn_b,
              moe_w_group, moe_b_group, moe_w_expert, moe_b_expert, moe_w_gate, moe_w_up,
              moe_w_down):
    bsz, seq, _ = x.shape
    for layer in range(DEPTH):
        i = layer // 2
        if layer % 2 == 0:
            h = x @ ab_w_in[i]
            gate_b = h[..., 0:CONV_CH]
            gate_c = h[..., CONV_CH:2 * CONV_CH]
            xv = h[..., 2 * CONV_CH:3 * CONV_CH]
            off = 3 * CONV_CH
            q = h[..., off:off + FOX_WIDTH].reshape(bsz, seq, FOX_HEADS, HEAD_DIM)
            k = h[..., off + FOX_WIDTH:off + 2 * FOX_WIDTH].reshape(bsz, seq, FOX_HEADS, HEAD_DIM)
            v = h[..., off + 2 * FOX_WIDTH:off + 3 * FOX_WIDTH].reshape(bsz, seq, FOX_HEADS, HEAD_DIM)
            f_logit = h[..., off + 3 * FOX_WIDTH:] + ab_b_forget[i]
            a_out = short_gated_conv(xv, gate_b, gate_c, ab_conv_w[i])
            b_out = forgetting_attention(q, k, v, f_logit).reshape(bsz, seq, FOX_WIDTH)
            mix = jnp.concatenate([a_out, b_out], axis=-1) @ ab_w_out[i]
        else:
            h = x @ c_w_in[i]
            q = h[..., 0:D_MODEL].reshape(bsz, seq, DIFF_HEADS, 2, HEAD_DIM)
            k = h[..., D_MODEL:2 * D_MODEL].reshape(bsz, seq, DIFF_HEADS, 2, HEAD_DIM)
            v = h[..., 2 * D_MODEL:].reshape(bsz, seq, DIFF_HEADS, 2 * HEAD_DIM)
            lam_init = 0.8 - 0.6 * math.exp(-0.3 * layer)
            lam = (jnp.exp(jnp.sum(c_lam_q1[i].astype(jnp.float32) * c_lam_k1[i].astype(jnp.float32)))
                   - jnp.exp(jnp.sum(c_lam_q2[i].astype(jnp.float32) * c_lam_k2[i].astype(jnp.float32)))
                   + lam_init)
            o = differential_attention(q[..., 0, :], q[..., 1, :], k[..., 0, :], k[..., 1, :], v,
                                       lam, lam_init, c_subln_g[i])
            mix = o.reshape(bsz, seq, D_MODEL) @ c_w_out[i]
        x = layer_norm(ALPHA * x + mix, ln_mix_g[layer], ln_mix_b[layer])
        ffn = hierarchical_moe(x, moe_w_group[layer], moe_b_group[layer], moe_w_expert[layer],
                               moe_b_expert[layer], moe_w_gate[layer], moe_w_up[layer],
                               moe_w_down[layer])
        x = layer_norm(ALPHA * x + ffn, ln_ffn_g[layer], ln_ffn_b[layer])
    return x
```

```python
import functools
import math

import numpy as np
import jax
import jax.numpy as jnp
from jax import lax
from jax.experimental import pallas as pl
from jax.experimental.pallas import tpu as pltpu

F32 = jnp.float32
BF16 = jnp.bfloat16

LANES = 128
HEAD_DIM = 64
N_GROUPS = 4
EXPERTS_PER_GROUP = 8
N_EXPERTS = N_GROUPS * EXPERTS_PER_GROUP
CHUNK = 64
LN_EPS = 1e-5
RMS_EPS = 1e-5
NEG_INF = -1e30
VMEM_LIMIT = 48 * 1024 * 1024

ROW_TILE = 256
PROJ_TILE = 512
ATTN_TILE = 256
MOE_ROWS = 256
COL_CHUNK = 512


def _params(n_grid):
    return pltpu.CompilerParams(dimension_semantics=("arbitrary",) * n_grid,
                                vmem_limit_bytes=VMEM_LIMIT)


def _layer_norm(z, g, b):
    mu = jnp.mean(z, axis=-1, keepdims=True)
    zc = z - mu
    var = jnp.mean(zc * zc, axis=-1, keepdims=True)
    return zc * lax.rsqrt(var + LN_EPS) * g + b


def _inproj_even_kernel(x_ref, w_ref, wft_ref, bf_ref, gates_ref, qkv_ref, ct_ref, carry_ref,
                        *, tiles_per_seq):
    i = pl.program_id(0)
    x = x_ref[...].astype(BF16)
    n_gate = gates_ref.shape[1]
    for j in range(n_gate // COL_CHUNK):
        sl = slice(j * COL_CHUNK, (j + 1) * COL_CHUNK)
        gates_ref[:, sl] = jnp.dot(x, w_ref[:, sl], preferred_element_type=F32)
    for j in range(qkv_ref.shape[1] // COL_CHUNK):
        sl = slice(j * COL_CHUNK, (j + 1) * COL_CHUNK)
        wsl = slice(n_gate + j * COL_CHUNK, n_gate + (j + 1) * COL_CHUNK)
        qkv_ref[:, sl] = jnp.dot(x, w_ref[:, wsl], preferred_element_type=F32).astype(BF16)

    z = lax.dot_general(wft_ref[...], x, (((1,), (1,)), ((), ())),
                        preferred_element_type=F32) + bf_ref[...]
    lf = jnp.minimum(z, 0.0) - jnp.log1p(jnp.exp(-jnp.abs(z)))
    tm = lf.shape[1]
    r = lax.broadcasted_iota(jnp.int32, (tm, tm), 0)
    c = lax.broadcasted_iota(jnp.int32, (tm, tm), 1)
    tri = jnp.where(r <= c, 1.0, 0.0).astype(BF16)
    a0 = lf.astype(BF16)
    r1 = lf - a0.astype(F32)
    a1 = r1.astype(BF16)
    a2 = (r1 - a1.astype(F32)).astype(BF16)
    cs = (jnp.dot(a0, tri, preferred_element_type=F32)
          + jnp.dot(a1, tri, preferred_element_type=F32)
          + jnp.dot(a2, tri, preferred_element_type=F32))

    @pl.when(i % tiles_per_seq == 0)
    def _():
        carry_ref[...] = jnp.zeros_like(carry_ref)

    cs = cs + carry_ref[...]
    ct_ref[0] = cs
    carry_ref[...] = cs[:, tm - 1:tm]


def _inproj_even(xf, w_main, wft, bf, *, batch, seq):
    n, d = xf.shape
    tm = min(PROJ_TILE, seq)
    tps = seq // tm
    n_heads = wft.shape[0]
    n_gate = 3 * (d // 2)
    n_qkv = w_main.shape[1] - n_gate
    return pl.pallas_call(
        functools.partial(_inproj_even_kernel, tiles_per_seq=tps),
        out_shape=(jax.ShapeDtypeStruct((n, n_gate), F32),
                   jax.ShapeDtypeStruct((n, n_qkv), BF16),
                   jax.ShapeDtypeStruct((batch, n_heads, seq), F32)),
        grid=(n // tm,),
        in_specs=[pl.BlockSpec((tm, d), lambda i: (i, 0)),
                  pl.BlockSpec(w_main.shape, lambda i: (0, 0)),
                  pl.BlockSpec(wft.shape, lambda i: (0, 0)),
                  pl.BlockSpec(bf.shape, lambda i: (0, 0))],
        out_specs=(pl.BlockSpec((tm, n_gate), lambda i: (i, 0)),
                   pl.BlockSpec((tm, n_qkv), lambda i: (i, 0)),
                   pl.BlockSpec((1, n_heads, tm), lambda i: (i // tps, 0, i % tps))),
        scratch_shapes=[pltpu.VMEM((n_heads, 1), F32)],
        compiler_params=_params(1),
        name="inproj_even",
    )(xf, w_main, wft, bf)


def _inproj_odd_kernel(x_ref, w_ref, h_ref):
    x = x_ref[...].astype(BF16)
    for j in range(h_ref.shape[1] // COL_CHUNK):
        sl = slice(j * COL_CHUNK, (j + 1) * COL_CHUNK)
        h_ref[:, sl] = jnp.dot(x, w_ref[:, sl], preferred_element_type=F32).astype(BF16)


def _inproj_odd(xf, w):
    n, d = xf.shape
    tm = PROJ_TILE
    return pl.pallas_call(
        _inproj_odd_kernel,
        out_shape=jax.ShapeDtypeStruct((n, w.shape[1]), BF16),
        grid=(n // tm,),
        in_specs=[pl.BlockSpec((tm, d), lambda i: (i, 0)),
                  pl.BlockSpec(w.shape, lambda i: (0, 0))],
        out_specs=pl.BlockSpec((tm, w.shape[1]), lambda i: (i, 0)),
        compiler_params=_params(1),
        name="inproj_odd",
    )(xf, w)


def _flash_pair(q_ref, k_ref, v_ref, qi, tq, bias_fn):
    q = q_ref[...]
    lane = lax.broadcasted_iota(jnp.int32, q.shape, 1)
    zero = jnp.zeros_like(q)
    q_subs = (jnp.where(lane < HEAD_DIM, q, zero), jnp.where(lane >= HEAD_DIM, q, zero))

    def step(j, carry, diag):
        off = pl.multiple_of(j * tq, tq)
        k = k_ref[pl.ds(off, tq), :]
        v = v_ref[pl.ds(off, tq), :]
        out = []
        for sub in range(2):
            m, l, acc = carry[3 * sub:3 * sub + 3]
            s = lax.dot_general(q_subs[sub], k, (((1,), (1,)), ((), ())),
                                preferred_element_type=F32)
            s = bias_fn(s, sub, j, diag)
            m_new = jnp.maximum(m, jnp.max(s, axis=-1, keepdims=True))
            alpha = jnp.exp(m - m_new)
            p = jnp.exp(s - m_new)
            l = alpha * l + jnp.sum(p, axis=-1, keepdims=True)
            acc = alpha * acc + jnp.dot(p.astype(BF16), v, preferred_element_type=F32)
            out += [m_new, l, acc]
        return tuple(out)

    init = (jnp.full((tq, 1), NEG_INF, F32), jnp.zeros((tq, 1), F32),
            jnp.zeros((tq, LANES), F32)) * 2
    carry = lax.fori_loop(0, qi, lambda j, c: step(j, c, False), init)
    return step(qi, carry, True)


def _fox_kernel(q_ref, k_ref, v_ref, c_ref, o_ref, *, tq):
    hp = pl.program_id(1)
    qi = pl.program_id(2)
    q_off = pl.multiple_of(qi * tq, tq)
    row = lax.broadcasted_iota(jnp.int32, (tq, tq), 0)
    col = lax.broadcasted_iota(jnp.int32, (tq, tq), 1)
    causal = col <= row

    def bias_fn(s, sub, j, diag):
        head = 2 * hp + sub
        c0 = c_ref[0, pl.ds(head, 1), pl.ds(q_off, tq)][:, 0:1]
        ck = c_ref[0, pl.ds(head, 1), pl.ds(pl.multiple_of(j * tq, tq), tq)]
        s = s + (c0 - ck)
        return jnp.where(causal, s, NEG_INF) if diag else s

    _, l0, acc0, _, l1, acc1 = _flash_pair(q_ref, k_ref, v_ref, qi, tq, bias_fn)
    lane = lax.broadcasted_iota(jnp.int32, (tq, LANES), 1)
    o_ref[...] = jnp.where(lane < HEAD_DIM, acc0 / l0, acc1 / l1).astype(o_ref.dtype)


def _fox_attention(qkv, ct, *, batch, seq):
    n, width3 = qkv.shape
    n_pairs = width3 // 3 // LANES
    tq = min(ATTN_TILE, seq)
    nq = seq // tq
    return pl.pallas_call(
        functools.partial(_fox_kernel, tq=tq),
        out_shape=jax.ShapeDtypeStruct((n, n_pairs * LANES), BF16),
        grid=(batch, n_pairs, nq),
        in_specs=[pl.BlockSpec((tq, LANES), lambda b, h, i: (b * nq + i, h)),
                  pl.BlockSpec((seq, LANES), lambda b, h, i: (b, n_pairs + h)),
                  pl.BlockSpec((seq, LANES), lambda b, h, i: (b, 2 * n_pairs + h)),
                  pl.BlockSpec((1,) + ct.shape[1:], lambda b, h, i: (b, 0, 0))],
        out_specs=pl.BlockSpec((tq, LANES), lambda b, h, i: (b * nq + i, h)),
        compiler_params=_params(3),
        name="fox_attention",
    )(qkv, qkv, qkv, ct)


def _diff_kernel(slopes_ref, q_ref, k_ref, v_ref, lam_ref, g_ref, o_ref, *, tq, lam_init):
    h = pl.program_id(1)
    qi = pl.program_id(2)
    slope = slopes_ref[h]
    row = lax.broadcasted_iota(jnp.int32, (tq, tq), 0)
    col = lax.broadcasted_iota(jnp.int32, (tq, tq), 1)
    dist = (row - col).astype(F32)
    visible = (col // CHUNK) <= (row // CHUNK)

    def bias_fn(s, sub, j, diag):
        s = s - slope * jnp.abs(dist + ((qi - j) * tq).astype(F32))
        return jnp.where(visible, s, NEG_INF) if diag else s

    _, l0, acc0, _, l1, acc1 = _flash_pair(q_ref, k_ref, v_ref, qi, tq, bias_fn)
    t1 = jnp.sum(lam_ref[0:1, :] * lam_ref[1:2, :], axis=-1, keepdims=True)
    t2 = jnp.sum(lam_ref[2:3, :] * lam_ref[3:4, :], axis=-1, keepdims=True)
    lam = jnp.exp(t1) - jnp.exp(t2) + lam_init
    o = acc0 / l0 - lam * (acc1 / l1)
    o = o * lax.rsqrt(jnp.mean(o * o, axis=-1, keepdims=True) + RMS_EPS)
    o_ref[...] = (o * g_ref[...] * (1.0 - lam_init)).astype(o_ref.dtype)


def _diff_attention(h, slopes, lam_rows, subln_g, *, batch, seq, lam_init):
    n, width3 = h.shape
    n_heads = width3 // 3 // LANES
    tq = min(ATTN_TILE, seq)
    nq = seq // tq
    grid_spec = pltpu.PrefetchScalarGridSpec(
        num_scalar_prefetch=1,
        grid=(batch, n_heads, nq),
        in_specs=[pl.BlockSpec((tq, LANES), lambda b, hd, i, s: (b * nq + i, hd)),
                  pl.BlockSpec((seq, LANES), lambda b, hd, i, s: (b, n_heads + hd)),
                  pl.BlockSpec((seq, LANES), lambda b, hd, i, s: (b, 2 * n_heads + hd)),
                  pl.BlockSpec(lam_rows.shape, lambda b, hd, i, s: (0, 0)),
                  pl.BlockSpec(subln_g.shape, lambda b, hd, i, s: (0, 0))],
        out_specs=pl.BlockSpec((tq, LANES), lambda b, hd, i, s: (b * nq + i, hd)),
    )
    return pl.pallas_call(
        functools.partial(_diff_kernel, tq=tq, lam_init=lam_init),
        out_shape=jax.ShapeDtypeStruct((n, n_heads * LANES), BF16),
        grid_spec=grid_spec,
        compiler_params=_params(3),
        name="diff_attention",
    )(slopes, h, h, h, lam_rows, subln_g)


def _route(lg):
    lane = lax.broadcasted_iota(jnp.int32, lg.shape, 1).astype(F32)
    low = jnp.float32(-3e38)
    is_g = lane < N_GROUPS
    gmax = jnp.max(jnp.where(is_g, lg, low), axis=-1, keepdims=True)
    gsel = jnp.min(jnp.where(is_g & (lg == gmax), lane, float(LANES)), axis=-1, keepdims=True)
    gsum = jnp.sum(jnp.where(is_g, jnp.exp(lg - gmax), 0.0), axis=-1, keepdims=True)
    gw = 1.0 / gsum
    lo = N_GROUPS + EXPERTS_PER_GROUP * gsel
    in_grp = (lane >= lo) & (lane < lo + EXPERTS_PER_GROUP)
    el = jnp.where(in_grp, lg, low)
    v1 = jnp.max(el, axis=-1, keepdims=True)
    i1 = jnp.min(jnp.where(in_grp & (lg == v1), lane, float(LANES)), axis=-1, keepdims=True)
    rest = in_grp & (lane != i1)
    v2 = jnp.max(jnp.where(rest, lg, low), axis=-1, keepdims=True)
    i2 = jnp.min(jnp.where(rest & (lg == v2), lane, float(LANES)), axis=-1, keepdims=True)
    t = jnp.exp(v2 - v1)
    w1 = gw / (1.0 + t)
    w2 = gw * (t / (1.0 + t))
    rec = jnp.where(lane == 0.0, i1 - N_GROUPS,
                    jnp.where(lane == 1.0, i2 - N_GROUPS,
                              jnp.where(lane == 2.0, w1, jnp.where(lane == 3.0, w2, 0.0))))
    return rec


def _mix_ln_route(mix, x_ref, lng_ref, lnb_ref, wr_ref, br_ref, xo_ref, rec_ref, alpha):
    xo = _layer_norm(alpha * x_ref[...] + mix, lng_ref[...], lnb_ref[...])
    xo_ref[...] = xo
    lg = jnp.dot(xo, wr_ref[...], preferred_element_type=F32,
                 precision=lax.Precision.HIGHEST) + br_ref[...]
    rec_ref[...] = _route(lg)


def _outproj_even_kernel(gates_ref, attn_ref, x_ref, wout_ref, convw_ref, lng_ref, lnb_ref,
                         wr_ref, br_ref, xo_ref, rec_ref, tail_ref, *, tiles_per_seq, alpha):
    i = pl.program_id(0)
    tm = x_ref.shape[0]
    cw = gates_ref.shape[1] // 3
    gb = gates_ref[:, 0:cw]
    u = gates_ref[:, cw:2 * cw] * gates_ref[:, 2 * cw:3 * cw]

    @pl.when(i % tiles_per_seq == 0)
    def _():
        tail_ref[...] = jnp.zeros_like(tail_ref)

    tail = tail_ref[...]
    row = lax.broadcasted_iota(jnp.int32, (tm, 1), 0)
    u1 = jnp.where(row == 0, tail[7:8, :], pltpu.roll(u, 1, axis=0))
    u2 = jnp.where(row == 0, tail[6:7, :],
                   jnp.where(row == 1, tail[7:8, :], pltpu.roll(u, 2, axis=0)))
    tail_ref[...] = u[tm - 8:tm, :]
    y = u2 * convw_ref[0:1, :] + u1 * convw_ref[1:2, :] + u * convw_ref[2:3, :]
    a = (gb * y).astype(BF16)
    mix = (jnp.dot(a, wout_ref[0:cw, :], preferred_element_type=F32)
           + jnp.dot(attn_ref[...], wout_ref[cw:, :], preferred_element_type=F32))
    _mix_ln_route(mix, x_ref, lng_ref, lnb_ref, wr_ref, br_ref, xo_ref, rec_ref, alpha)


def _outproj_odd_kernel(attn_ref, x_ref, wout_ref, lng_ref, lnb_ref, wr_ref, br_ref,
                        xo_ref, rec_ref, *, alpha):
    mix = jnp.dot(attn_ref[...], wout_ref[...], preferred_element_type=F32)
    _mix_ln_route(mix, x_ref, lng_ref, lnb_ref, wr_ref, br_ref, xo_ref, rec_ref, alpha)


def _outproj(acts, xf, wout, convw, lng, lnb, wr, br, *, seq, alpha):
    n, d = xf.shape
    tm = min(ROW_TILE, seq)
    row_spec = lambda width: pl.BlockSpec((tm, width), lambda i: (i, 0))
    full = lambda a: pl.BlockSpec(a.shape, lambda i: (0, 0))
    even = convw is not None
    if even:
        body = functools.partial(_outproj_even_kernel, tiles_per_seq=seq // tm, alpha=alpha)
        operands = (*acts, xf, wout, convw, lng, lnb, wr, br)
        scratch = [pltpu.VMEM((8, acts[0].shape[1] // 3), F32)]
    else:
        body = functools.partial(_outproj_odd_kernel, alpha=alpha)
        operands = (*acts, xf, wout, lng, lnb, wr, br)
        scratch = []
    in_specs = [row_spec(a.shape[1]) for a in acts] + [row_spec(d)]
    in_specs += [full(a) for a in operands[len(acts) + 1:]]
    return pl.pallas_call(
        body,
        out_shape=(jax.ShapeDtypeStruct((n, d), F32), jax.ShapeDtypeStruct((n, LANES), F32)),
        grid=(n // tm,),
        in_specs=in_specs,
        out_specs=(row_spec(d), row_spec(LANES)),
        scratch_shapes=scratch,
        compiler_params=_params(1),
        name="outproj_even" if even else "outproj_odd",
    )(*operands)


def _dispatch_plan(rec, n_tok, rows):
    eid = rec[:, 0:2].astype(jnp.int32).reshape(-1)
    wt = rec[:, 2:4].reshape(-1)
    onehot = (eid[:, None] == jnp.arange(N_EXPERTS, dtype=jnp.int32)[None, :]).astype(jnp.int32)
    csum = jnp.cumsum(onehot, axis=0)
    rank = jnp.sum(csum * onehot, axis=1) - 1
    counts = csum[-1]
    padded = (counts + rows - 1) // rows * rows
    pad_end = jnp.cumsum(padded)
    pos = (pad_end - padded)[eid] + rank
    cap = 2 * n_tok + N_EXPERTS * rows
    n_blocks = cap // rows
    row_tok = jnp.zeros((cap,), jnp.int32).at[pos].set(jnp.arange(2 * n_tok, dtype=jnp.int32) // 2)
    row_w = jnp.zeros((cap,), F32).at[pos].set(wt)
    blk_e = jnp.minimum(jnp.searchsorted(pad_end, jnp.arange(n_blocks, dtype=jnp.int32) * rows,
                                         side="right"), N_EXPERTS - 1).astype(jnp.int32)
    n_used = (pad_end[-1] // rows).astype(jnp.int32).reshape(1)
    return pos.reshape(n_tok, 2), row_tok, row_w, blk_e, n_used


def _row_copy(src_hbm, dst_ref, src_row, dst_row, sem):
    return pltpu.make_async_copy(src_hbm.at[pl.ds(src_row, 1)], dst_ref.at[pl.ds(dst_row, 1)], sem)


def _gather_rows_kernel(n_used_ref, idx_ref, x_hbm, out_ref, sem):
    i = pl.program_id(0)
    rows = out_ref.shape[0]

    @pl.when(i < n_used_ref[0])
    def _():
        def issue(r, carry):
            _row_copy(x_hbm, out_ref, idx_ref[0, 0, r], r, sem).start()
            return carry

        lax.fori_loop(0, rows, issue, 0)
        pltpu.make_async_copy(x_hbm.at[pl.ds(0, rows)], out_ref, sem).wait()

    @pl.when(i >= n_used_ref[0])
    def _():
        out_ref[...] = jnp.zeros_like(out_ref)


def _gather_rows(xf, row_tok, n_used, rows):
    cap = row_tok.shape[0]
    n_blocks = cap // rows
    d = xf.shape[1]
    grid_spec = pltpu.PrefetchScalarGridSpec(
        num_scalar_prefetch=1,
        grid=(n_blocks,),
        in_specs=[pl.BlockSpec((1, 1, rows), lambda i, nu: (i, 0, 0), memory_space=pltpu.SMEM),
                  pl.BlockSpec(memory_space=pl.ANY)],
        out_specs=pl.BlockSpec((rows, d), lambda i, nu: (i, 0)),
        scratch_shapes=[pltpu.SemaphoreType.DMA(())],
    )
    return pl.pallas_call(
        _gather_rows_kernel,
        out_shape=jax.ShapeDtypeStruct((cap, d), xf.dtype),
        grid_spec=grid_spec,
        compiler_params=_params(1),
        name="moe_gather",
    )(n_used, row_tok.reshape(n_blocks, 1, rows), xf)


def _expert_mlp_kernel(blk_e_ref, n_used_ref, xb_ref, wg_ref, wu_ref, wd_ref, roww_ref, out_ref,
                       wgu_s, wd_s):
    i = pl.program_id(0)
    dh = wg_ref.shape[2]
    changed = jnp.logical_or(i == 0, blk_e_ref[i] != blk_e_ref[jnp.maximum(i - 1, 0)])

    @pl.when(changed)
    def _():
        wgu_s[:, 0:dh] = wg_ref[0].astype(BF16)
        wgu_s[:, dh:2 * dh] = wu_ref[0].astype(BF16)
        wd_s[...] = wd_ref[0].astype(BF16)

    @pl.when(i < n_used_ref[0])
    def _():
        x = xb_ref[...].astype(BF16)
        gu = jnp.dot(x, wgu_s[...], preferred_element_type=F32)
        g = gu[:, 0:dh]
        hid = (g / (1.0 + jnp.exp(-g))) * gu[:, dh:2 * dh]
        y = jnp.dot(hid.astype(BF16), wd_s[...], preferred_element_type=F32)
        out_ref[...] = y * roww_ref[...]

    @pl.when(i >= n_used_ref[0])
    def _():
        out_ref[...] = jnp.zeros_like(out_ref)


def _expert_mlp(xb, w_gate, w_up, w_down, row_w, blk_e, n_used, rows):
    cap, d = xb.shape
    dh = w_gate.shape[2]
    grid_spec = pltpu.PrefetchScalarGridSpec(
        num_scalar_prefetch=2,
        grid=(cap // rows,),
        in_specs=[pl.BlockSpec((rows, d), lambda i, be, nu: (i, 0)),
                  pl.BlockSpec((1, d, dh), lambda i, be, nu: (be[i], 0, 0)),
                  pl.BlockSpec((1, d, dh), lambda i, be, nu: (be[i], 0, 0)),
                  pl.BlockSpec((1, dh, d), lambda i, be, nu: (be[i], 0, 0)),
                  pl.BlockSpec((rows, 1), lambda i, be, nu: (i, 0))],
        out_specs=pl.BlockSpec((rows, d), lambda i, be, nu: (i, 0)),
        scratch_shapes=[pltpu.VMEM((d, 2 * dh), BF16), pltpu.VMEM((dh, d), BF16)],
    )
    return pl.pallas_call(
        _expert_mlp_kernel,
        out_shape=jax.ShapeDtypeStruct((cap, d), F32),
        grid_spec=grid_spec,
        compiler_params=_params(1),
        name="moe_experts",
    )(blk_e, n_used, xb, w_gate, w_up, w_down, row_w.reshape(cap, 1))


def _combine_ln_kernel(pos_ref, yb_hbm, x_ref, lng_ref, lnb_ref, xo_ref, ya_ref, yb_ref, sem,
                       *, alpha):
    tm = x_ref.shape[0]

    def issue(r, carry):
        _row_copy(yb_hbm, ya_ref, pos_ref[0, 0, r], r, sem).start()
        _row_copy(yb_hbm, yb_ref, pos_ref[0, 0, tm + r], r, sem).start()
        return carry

    lax.fori_loop(0, tm, issue, 0)
    pltpu.make_async_copy(yb_hbm.at[pl.ds(0, tm)], ya_ref, sem).wait()
    pltpu.make_async_copy(yb_hbm.at[pl.ds(0, tm)], yb_ref, sem).wait()
    ffn = ya_ref[...] + yb_ref[...]
    xo_ref[...] = _layer_norm(alpha * x_ref[...] + ffn, lng_ref[...], lnb_ref[...])


def _combine_ln(yb, pos, xf, lng, lnb, *, alpha):
    n, d = xf.shape
    tm = ROW_TILE
    n_tiles = n // tm
    pos_tiles = jnp.transpose(pos.reshape(n_tiles, tm, 2), (0, 2, 1)).reshape(n_tiles, 1, 2 * tm)
    return pl.pallas_call(
        functools.partial(_combine_ln_kernel, alpha=alpha),
        out_shape=jax.ShapeDtypeStruct((n, d), F32),
        grid=(n_tiles,),
        in_specs=[pl.BlockSpec((1, 1, 2 * tm), lambda i: (i, 0, 0), memory_space=pltpu.SMEM),
                  pl.BlockSpec(memory_space=pl.ANY),
                  pl.BlockSpec((tm, d), lambda i: (i, 0)),
                  pl.BlockSpec(lng.shape, lambda i: (0, 0)),
                  pl.BlockSpec(lnb.shape, lambda i: (0, 0))],
        out_specs=pl.BlockSpec((tm, d), lambda i: (i, 0)),
        scratch_shapes=[pltpu.VMEM((tm, d), F32), pltpu.VMEM((tm, d), F32),
                        pltpu.SemaphoreType.DMA(())],
        compiler_params=_params(1),
        name="moe_combine_ln",
    )(pos_tiles, yb, xf, lng, lnb)


def kernel(x, ab_w_in, ab_b_forget, ab_conv_w, ab_w_out, c_w_in, c_lam_q1, c_lam_k1, c_lam_q2,
           c_lam_k2, c_subln_g, c_w_out, ln_mix_g, ln_mix_b, ln_ffn_g, ln_ffn_b, moe_w_group,
           moe_b_group, moe_w_expert, moe_b_expert, moe_w_gate, moe_w_up, moe_w_down):
    batch, seq, d = x.shape
    depth = ln_mix_g.shape[0]
    n_tok = batch * seq
    alpha = (2.0 * depth) ** 0.25
    scale = HEAD_DIM ** -0.5
    conv_ch = ab_conv_w.shape[1]
    fox_heads = ab_b_forget.shape[1]
    n_diff_heads = d // (2 * HEAD_DIM)
    slopes = jnp.asarray(2.0 ** (-8.0 * np.arange(1, n_diff_heads + 1) / n_diff_heads), F32)
    xf = x.reshape(n_tok, d)

    def pad_lanes(v):
        return jnp.pad(v, (0, LANES - v.shape[0]))

    for layer in range(depth):
        i = layer // 2
        w_router = jnp.pad(jnp.concatenate([moe_w_group[layer], moe_w_expert[layer]], axis=1),
                           ((0, 0), (0, LANES - N_GROUPS - N_EXPERTS)))
        b_router = pad_lanes(jnp.concatenate([moe_b_group[layer], moe_b_expert[layer]]))[None, :]
        lng, lnb = ln_mix_g[layer][None, :], ln_mix_b[layer][None, :]
        if layer % 2 == 0:
            w = ab_w_in[i]
            q_lo = 3 * conv_ch
            q_hi = q_lo + fox_heads * HEAD_DIM
            w = w.at[:, q_lo:q_hi].multiply(scale)
            n_main = w.shape[1] - fox_heads
            w_main = w[:, :n_main].astype(BF16)
            wft = jnp.transpose(w[:, n_main:]).astype(BF16)
            gates, qkv, ct = _inproj_even(xf, w_main, wft, ab_b_forget[i][:, None],
                                          batch=batch, seq=seq)
            attn = _fox_attention(qkv, ct, batch=batch, seq=seq)
            convw = jnp.pad(jnp.transpose(ab_conv_w[i]), ((0, 8 - ab_conv_w.shape[2]), (0, 0)))
            xf, rec = _outproj((gates, attn), xf, ab_w_out[i].astype(BF16), convw, lng, lnb,
                               w_router, b_router, seq=seq, alpha=alpha)
        else:
            w = c_w_in[i].at[:, 0:d].multiply(scale)
            h = _inproj_odd(xf, w.astype(BF16))
            lam_init = 0.8 - 0.6 * math.exp(-0.3 * layer)
            lam_rows = jnp.pad(jnp.stack([pad_lanes(c_lam_q1[i]), pad_lanes(c_lam_k1[i]),
                                          pad_lanes(c_lam_q2[i]), pad_lanes(c_lam_k2[i])]),
                               ((0, 4), (0, 0)))
            attn = _diff_attention(h, slopes, lam_rows, c_subln_g[i][None, :],
                                   batch=batch, seq=seq, lam_init=lam_init)
            xf, rec = _outproj((attn,), xf, c_w_out[i].astype(BF16), None, lng, lnb,
                               w_router, b_router, seq=seq, alpha=alpha)

        pos, row_tok, row_w, blk_e, n_used = _dispatch_plan(rec, n_tok, MOE_ROWS)
        xb = _gather_rows(xf, row_tok, n_used, MOE_ROWS)
        yb = _expert_mlp(xb, moe_w_gate[layer], moe_w_up[layer], moe_w_down[layer], row_w,
                         blk_e, n_used, MOE_ROWS)
        xf = _combine_ln(yb, pos, xf, ln_ffn_g[layer][None, :], ln_ffn_b[layer][None, :],
                         alpha=alpha)
    return xf.reshape(batch, seq, d)
```

```python
import functools
import math

import numpy as np
import jax
import jax.numpy as jnp
from jax import lax
from jax.experimental import pallas as pl
from jax.experimental.pallas import tpu as pltpu

F32 = jnp.float32
BF16 = jnp.bfloat16

LANES = 128
HEAD_DIM = 64
N_GROUPS = 4
EXPERTS_PER_GROUP = 8
N_EXPERTS = N_GROUPS * EXPERTS_PER_GROUP
CHUNK = 64
LN_EPS = 1e-5
RMS_EPS = 1e-5
LOG2E = math.log2(math.e)
M_INIT = -1e30
MASKED = -3e38
VMEM_LIMIT = 48 * 1024 * 1024

ROW_TILE = 256
PROJ_TILE = 512
ATTN_TILE = 256
SOFTMAX_ROWS = 32
MOE_ROWS = 256
COL_CHUNK = 512

_NT = (((1,), (1,)), ((), ()))


def _params(n_grid):
    return pltpu.CompilerParams(dimension_semantics=("arbitrary",) * n_grid,
                                vmem_limit_bytes=VMEM_LIMIT)


def _layer_norm(z, g, b):
    mu = jnp.mean(z, axis=-1, keepdims=True)
    zc = z - mu
    var = jnp.mean(zc * zc, axis=-1, keepdims=True)
    return zc * lax.rsqrt(var + LN_EPS) * g + b


def _split3(v):
    a0 = v.astype(BF16)
    r1 = v - a0.astype(F32)
    a1 = r1.astype(BF16)
    a2 = (r1 - a1.astype(F32)).astype(BF16)
    return a0, a1, a2


def _inproj_even_kernel(x_ref, w_ref, wf_ref, bf_ref, gates_ref, qkv_ref, csplit_ref, carry_ref,
                        *, tiles_per_seq):
    i = pl.program_id(0)
    x = x_ref[...].astype(BF16)
    n_gate = gates_ref.shape[1]
    for j in range(n_gate // COL_CHUNK):
        sl = slice(j * COL_CHUNK, (j + 1) * COL_CHUNK)
        gates_ref[:, sl] = jnp.dot(x, w_ref[:, sl], preferred_element_type=F32)
    for j in range(qkv_ref.shape[1] // COL_CHUNK):
        sl = slice(j * COL_CHUNK, (j + 1) * COL_CHUNK)
        wsl = slice(n_gate + j * COL_CHUNK, n_gate + (j + 1) * COL_CHUNK)
        qkv_ref[:, sl] = jnp.dot(x, w_ref[:, wsl], preferred_element_type=F32).astype(BF16)

    z = jnp.dot(x, wf_ref[...], preferred_element_type=F32) + bf_ref[...]
    lf = jnp.minimum(z, 0.0) - jnp.log1p(jnp.exp(-jnp.abs(z)))
    tm = lf.shape[0]
    r = lax.broadcasted_iota(jnp.int32, (tm, tm), 0)
    c = lax.broadcasted_iota(jnp.int32, (tm, tm), 1)
    tri = jnp.where(c <= r, 1.0, 0.0).astype(BF16)
    a0, a1, a2 = _split3(lf)
    cs = (jnp.dot(tri, a0, preferred_element_type=F32)
          + jnp.dot(tri, a1, preferred_element_type=F32)
          + jnp.dot(tri, a2, preferred_element_type=F32))

    @pl.when(i % tiles_per_seq == 0)
    def _():
        carry_ref[...] = jnp.zeros_like(carry_ref)

    cs = cs + carry_ref[0:1, :]
    carry_ref[...] = jnp.broadcast_to(cs[tm - 1:tm, :], carry_ref.shape)
    c0, c1, c2 = _split3(cs * LOG2E)
    lane = lax.broadcasted_iota(jnp.int32, (tm, LANES), 1)
    csplit_ref[:, 0:LANES] = c0
    csplit_ref[:, LANES:2 * LANES] = c1
    csplit_ref[:, 2 * LANES:3 * LANES] = c2
    csplit_ref[:, 3 * LANES:4 * LANES] = jnp.where(lane == 0, 1.0, 0.0).astype(BF16)


def _inproj_even(xf, w_main, wf, bf, *, seq):
    n, d = xf.shape
    tm = min(PROJ_TILE, seq)
    n_gate = 3 * (d // 2)
    n_qkv = w_main.shape[1] - n_gate
    return pl.pallas_call(
        functools.partial(_inproj_even_kernel, tiles_per_seq=seq // tm),
        out_shape=(jax.ShapeDtypeStruct((n, n_gate), F32),
                   jax.ShapeDtypeStruct((n, n_qkv), BF16),
                   jax.ShapeDtypeStruct((n, 4 * LANES), BF16)),
        grid=(n // tm,),
        in_specs=[pl.BlockSpec((tm, d), lambda i: (i, 0)),
                  pl.BlockSpec(w_main.shape, lambda i: (0, 0)),
                  pl.BlockSpec(wf.shape, lambda i: (0, 0)),
                  pl.BlockSpec(bf.shape, lambda i: (0, 0))],
        out_specs=(pl.BlockSpec((tm, n_gate), lambda i: (i, 0)),
                   pl.BlockSpec((tm, n_qkv), lambda i: (i, 0)),
                   pl.BlockSpec((tm, 4 * LANES), lambda i: (i, 0))),
        scratch_shapes=[pltpu.VMEM((8, LANES), F32)],
        compiler_params=_params(1),
        name="inproj_even",
    )(xf, w_main, wf, bf)


def _inproj_odd_kernel(x_ref, w_ref, h_ref):
    x = x_ref[...].astype(BF16)
    for j in range(h_ref.shape[1] // COL_CHUNK):
        sl = slice(j * COL_CHUNK, (j + 1) * COL_CHUNK)
        h_ref[:, sl] = jnp.dot(x, w_ref[:, sl], preferred_element_type=F32).astype(BF16)


def _inproj_odd(xf, w):
    n, d = xf.shape
    tm = PROJ_TILE
    return pl.pallas_call(
        _inproj_odd_kernel,
        out_shape=jax.ShapeDtypeStruct((n, w.shape[1]), BF16),
        grid=(n // tm,),
        in_specs=[pl.BlockSpec((tm, d), lambda i: (i, 0)),
                  pl.BlockSpec(w.shape, lambda i: (0, 0))],
        out_specs=pl.BlockSpec((tm, w.shape[1]), lambda i: (i, 0)),
        compiler_params=_params(1),
        name="inproj_odd",
    )(xf, w)


def _flash_scratch(tq, seq):
    r2 = 2 * tq
    return ([pltpu.VMEM((seq, 2 * LANES), BF16)]
            + [pltpu.VMEM((r2, tq), F32) for _ in range(2)]
            + [pltpu.VMEM((r2, tq), BF16) for _ in range(2)]
            + [pltpu.VMEM((r2, LANES), F32) for _ in range(2)]
            + [pltpu.VMEM((r2, LANES), F32) for _ in range(3)])


def _stack_queries(q, feat_a, feat_b):
    lane = lax.broadcasted_iota(jnp.int32, q.shape, 1)
    zero = jnp.zeros_like(q)
    qa = jnp.concatenate([jnp.where(lane < HEAD_DIM, q, zero), feat_a], axis=1)
    qb = jnp.concatenate([jnp.where(lane >= HEAD_DIM, q, zero), feat_b], axis=1)
    return jnp.concatenate([qa, qb], axis=0)


def _flash_pipeline(qaug, kaug_ref, v_ref, scratch, qi, tq, diag_bias_ref):
    s_x, s_y, p_x, p_y, al_x, al_y, m_scr, l_scr, acc_scr = scratch
    r2 = 2 * tq
    n_rep = tq // LANES
    m_scr[...] = jnp.full(m_scr.shape, M_INIT, F32)
    l_scr[...] = jnp.zeros(l_scr.shape, F32)
    acc_scr[...] = jnp.zeros(acc_scr.shape, F32)
    p_y[...] = jnp.zeros(p_y.shape, BF16)
    al_y[...] = jnp.ones(al_y.shape, F32)

    def scores(j, s_ref):
        off = pl.multiple_of(j * tq, tq)
        s_ref[...] = lax.dot_general(qaug, kaug_ref[pl.ds(off, tq), :], _NT,
                                     preferred_element_type=F32)

    def weighted_values(j, p_ref, al_ref):
        off = pl.multiple_of(j * tq, tq)
        acc_scr[...] = al_ref[...] * acc_scr[...] + jnp.dot(
            p_ref[...], v_ref[pl.ds(off, tq), :], preferred_element_type=F32)

    def softmax(s_ref, p_ref, al_ref, bias_fn):
        for c in range(r2 // SOFTMAX_ROWS):
            rows = slice(c * SOFTMAX_ROWS, (c + 1) * SOFTMAX_ROWS)
            x = s_ref[rows, :]
            if bias_fn is not None:
                x = x + bias_fn(c)
            m_old = m_scr[rows, :]
            m_new = jnp.maximum(m_old, jnp.max(x, axis=1, keepdims=True))
            alpha = jnp.exp2(m_old - m_new)
            p = jnp.exp2(x - jnp.concatenate([m_new] * n_rep, axis=1))
            p_sum = p[:, 0:LANES]
            for t in range(1, n_rep):
                p_sum = p_sum + p[:, t * LANES:(t + 1) * LANES]
            m_scr[rows, :] = m_new
            l_scr[rows, :] = alpha * l_scr[rows, :] + p_sum
            al_ref[rows, :] = alpha
            p_ref[rows, :] = p.astype(BF16)

    scores(0, s_x)

    def pair(i, carry):
        e = 2 * i
        weighted_values(jnp.maximum(e - 1, 0), p_y, al_y)
        scores(e + 1, s_y)
        softmax(s_x, p_x, al_x, None)
        weighted_values(e, p_x, al_x)
        scores(e + 2, s_x)
        softmax(s_y, p_y, al_y, None)
        return carry

    lax.fori_loop(0, qi // 2, pair, 0)

    e = 2 * (qi // 2)
    gate = jnp.where(qi % 2 == 1, 0.0, MASKED).astype(F32)
    weighted_values(jnp.maximum(e - 1, 0), p_y, al_y)
    scores(qi, s_y)
    softmax(s_x, p_x, al_x, lambda c: gate)
    weighted_values(e, p_x, al_x)

    def diag_bias(c):
        r0 = (c * SOFTMAX_ROWS) % tq
        return diag_bias_ref[r0:r0 + SOFTMAX_ROWS, :]

    softmax(s_y, p_y, al_y, diag_bias)
    weighted_values(qi, p_y, al_y)
    return jnp.sum(l_scr[...], axis=1, keepdims=True), acc_scr[...]


def _fox_kernel(q_ref, k_ref, v_ref, cs_ref, eq_ref, ek_ref, o_ref, diag_ref, kaug_ref, *scratch,
                tq):
    qi = pl.program_id(2)

    @pl.when(qi == 0)
    def _():
        kaug_ref[:, 0:LANES] = k_ref[...]
        kaug_ref[:, LANES:] = jnp.dot(cs_ref[...], ek_ref[0],
                                      preferred_element_type=F32).astype(BF16)

    row = lax.broadcasted_iota(jnp.int32, (tq, tq), 0)
    col = lax.broadcasted_iota(jnp.int32, (tq, tq), 1)
    diag_ref[...] = jnp.where(col <= row, 0.0, MASKED)
    cs_q = cs_ref[pl.ds(pl.multiple_of(qi * tq, tq), tq), :]
    feats = [jnp.dot(cs_q, eq_ref[0, sub], preferred_element_type=F32).astype(BF16)
             for sub in range(2)]
    qaug = _stack_queries(q_ref[...], feats[0], feats[1])
    l, acc = _flash_pipeline(qaug, kaug_ref, v_ref, scratch, qi, tq, diag_ref)
    out = acc / l
    lane = lax.broadcasted_iota(jnp.int32, (tq, LANES), 1)
    o_ref[...] = jnp.where(lane < HEAD_DIM, out[0:tq], out[tq:2 * tq]).astype(o_ref.dtype)


def _fox_feature_maps(n_pairs):
    one_row = 3 * LANES
    ek = np.zeros((n_pairs, 4 * LANES, LANES), np.float32)
    eq = np.zeros((n_pairs, 2, 4 * LANES, LANES), np.float32)
    for p in range(n_pairs):
        for sub in range(2):
            head, base = 2 * p + sub, 6 * sub
            for t in range(3):
                ek[p, one_row, base + t] = 1.0
                ek[p, t * LANES + head, base + 3 + t] = -1.0
                eq[p, sub, t * LANES + head, base + t] = 1.0
                eq[p, sub, one_row, base + 3 + t] = 1.0
    return jnp.asarray(eq, BF16), jnp.asarray(ek, BF16)


def _fox_attention(qkv, csplit, *, batch, seq):
    n, width3 = qkv.shape
    n_pairs = width3 // 3 // LANES
    tq = min(ATTN_TILE, seq)
    nq = seq // tq
    eq, ek = _fox_feature_maps(n_pairs)
    return pl.pallas_call(
        functools.partial(_fox_kernel, tq=tq),
        out_shape=jax.ShapeDtypeStruct((n, n_pairs * LANES), BF16),
        grid=(batch, n_pairs, nq),
        in_specs=[pl.BlockSpec((tq, LANES), lambda b, h, i: (b * nq + i, h)),
                  pl.BlockSpec((seq, LANES), lambda b, h, i: (b, n_pairs + h)),
                  pl.BlockSpec((seq, LANES), lambda b, h, i: (b, 2 * n_pairs + h)),
                  pl.BlockSpec((seq, csplit.shape[1]), lambda b, h, i: (b, 0)),
                  pl.BlockSpec((1,) + eq.shape[1:], lambda b, h, i: (h, 0, 0, 0)),
                  pl.BlockSpec((1,) + ek.shape[1:], lambda b, h, i: (h, 0, 0))],
        out_specs=pl.BlockSpec((tq, LANES), lambda b, h, i: (b * nq + i, h)),
        scratch_shapes=[pltpu.VMEM((tq, tq), F32)] + _flash_scratch(tq, seq),
        compiler_params=_params(3),
        name="fox_attention",
    )(qkv, qkv, qkv, csplit, eq, ek)


def _diff_kernel(slopes_ref, q_ref, k_ref, v_ref, lam_ref, g_ref, o_ref, diag_ref, kaug_ref,
                 *scratch, tq, lam_init):
    h = pl.program_id(1)
    qi = pl.program_id(2)
    seq = k_ref.shape[0]
    slope2 = slopes_ref[h] * LOG2E

    def bias_lanes(n_rows, first, base, other, other_value):
        lane = lax.broadcasted_iota(jnp.int32, (n_rows, LANES), 1)
        pos = (lax.broadcasted_iota(jnp.int32, (n_rows, LANES), 0) + first).astype(F32) * slope2
        terms = [t.astype(F32) for t in _split3(pos)]
        feat = jnp.where((lane >= other) & (lane < other + 3), other_value, 0.0)
        for t, term in enumerate(terms):
            feat = jnp.where(lane == base + t, term, feat)
        return feat.astype(BF16)

    @pl.when(qi == 0)
    def _():
        kaug_ref[:, 0:LANES] = k_ref[...]
        kaug_ref[:, LANES:] = bias_lanes(seq, 0, 0, 3, -1.0)

    qfeat = bias_lanes(tq, qi * tq, 3, 0, 1.0)
    row = lax.broadcasted_iota(jnp.int32, (tq, tq), 0)
    col = lax.broadcasted_iota(jnp.int32, (tq, tq), 1)
    ahead = jnp.maximum(col - row, 0).astype(F32)
    diag_ref[...] = jnp.where((col // CHUNK) <= (row // CHUNK), -2.0 * slope2 * ahead, MASKED)
    qaug = _stack_queries(q_ref[...], qfeat, qfeat)
    l, acc = _flash_pipeline(qaug, kaug_ref, v_ref, scratch, qi, tq, diag_ref)
    out = acc / l
    t1_ = jnp.sum(lam_ref[0:1, :] * lam_ref[1:2, :], axis=-1, keepdims=True)
    t2_ = jnp.sum(lam_ref[2:3, :] * lam_ref[3:4, :], axis=-1, keepdims=True)
    lam = jnp.exp(t1_) - jnp.exp(t2_) + lam_init
    o = out[0:tq] - lam * out[tq:2 * tq]
    o = o * lax.rsqrt(jnp.mean(o * o, axis=-1, keepdims=True) + RMS_EPS)
    o_ref[...] = (o * g_ref[...] * (1.0 - lam_init)).astype(o_ref.dtype)


def _diff_attention(h, slopes, lam_rows, subln_g, *, batch, seq, lam_init):
    n, width3 = h.shape
    n_heads = width3 // 3 // LANES
    tq = min(ATTN_TILE, seq)
    nq = seq // tq
    grid_spec = pltpu.PrefetchScalarGridSpec(
        num_scalar_prefetch=1,
        grid=(batch, n_heads, nq),
        in_specs=[pl.BlockSpec((tq, LANES), lambda b, hd, i, s: (b * nq + i, hd)),
                  pl.BlockSpec((seq, LANES), lambda b, hd, i, s: (b, n_heads + hd)),
                  pl.BlockSpec((seq, LANES), lambda b, hd, i, s: (b, 2 * n_heads + hd)),
                  pl.BlockSpec(lam_rows.shape, lambda b, hd, i, s: (0, 0)),
                  pl.BlockSpec(subln_g.shape, lambda b, hd, i, s: (0, 0))],
        out_specs=pl.BlockSpec((tq, LANES), lambda b, hd, i, s: (b * nq + i, hd)),
        scratch_shapes=[pltpu.VMEM((tq, tq), F32)] + _flash_scratch(tq, seq),
    )
    return pl.pallas_call(
        functools.partial(_diff_kernel, tq=tq, lam_init=lam_init),
        out_shape=jax.ShapeDtypeStruct((n, n_heads * LANES), BF16),
        grid_spec=grid_spec,
        compiler_params=_params(3),
        name="diff_attention",
    )(slopes, h, h, h, lam_rows, subln_g)


def _route(lg):
    lane = lax.broadcasted_iota(jnp.int32, lg.shape, 1).astype(F32)
    low = jnp.float32(-3e38)
    is_g = lane < N_GROUPS
    gmax = jnp.max(jnp.where(is_g, lg, low), axis=-1, keepdims=True)
    gsel = jnp.min(jnp.where(is_g & (lg == gmax), lane, float(LANES)), axis=-1, keepdims=True)
    gsum = jnp.sum(jnp.where(is_g, jnp.exp(lg - gmax), 0.0), axis=-1, keepdims=True)
    gw = 1.0 / gsum
    lo = N_GROUPS + EXPERTS_PER_GROUP * gsel
    in_grp = (lane >= lo) & (lane < lo + EXPERTS_PER_GROUP)
    el = jnp.where(in_grp, lg, low)
    v1 = jnp.max(el, axis=-1, keepdims=True)
    i1 = jnp.min(jnp.where(in_grp & (lg == v1), lane, float(LANES)), axis=-1, keepdims=True)
    rest = in_grp & (lane != i1)
    v2 = jnp.max(jnp.where(rest, lg, low), axis=-1, keepdims=True)
    i2 = jnp.min(jnp.where(rest & (lg == v2), lane, float(LANES)), axis=-1, keepdims=True)
    t = jnp.exp(v2 - v1)
    w1 = gw / (1.0 + t)
    w2 = gw * (t / (1.0 + t))
    rec = jnp.where(lane == 0.0, i1 - N_GROUPS,
                    jnp.where(lane == 1.0, i2 - N_GROUPS,
                              jnp.where(lane == 2.0, w1, jnp.where(lane == 3.0, w2, 0.0))))
    return rec


def _mix_ln_route(mix, x_ref, lng_ref, lnb_ref, wr_ref, br_ref, xo_ref, rec_ref, alpha):
    xo = _layer_norm(alpha * x_ref[...] + mix, lng_ref[...], lnb_ref[...])
    xo_ref[...] = xo
    lg = jnp.dot(xo, wr_ref[...], preferred_element_type=F32,
                 precision=lax.Precision.HIGHEST) + br_ref[...]
    rec_ref[...] = _route(lg)


def _outproj_even_kernel(gates_ref, attn_ref, x_ref, wout_ref, convw_ref, lng_ref, lnb_ref,
                         wr_ref, br_ref, xo_ref, rec_ref, tail_ref, *, tiles_per_seq, alpha):
    i = pl.program_id(0)
    tm = x_ref.shape[0]
    cw = gates_ref.shape[1] // 3
    gb = gates_ref[:, 0:cw]
    u = gates_ref[:, cw:2 * cw] * gates_ref[:, 2 * cw:3 * cw]

    @pl.when(i % tiles_per_seq == 0)
    def _():
        tail_ref[...] = jnp.zeros_like(tail_ref)

    tail = tail_ref[...]
    row = lax.broadcasted_iota(jnp.int32, (tm, 1), 0)
    u1 = jnp.where(row == 0, tail[7:8, :], pltpu.roll(u, 1, axis=0))
    u2 = jnp.where(row == 0, tail[6:7, :],
                   jnp.where(row == 1, tail[7:8, :], pltpu.roll(u, 2, axis=0)))
    tail_ref[...] = u[tm - 8:tm, :]
    y = u2 * convw_ref[0:1, :] + u1 * convw_ref[1:2, :] + u * convw_ref[2:3, :]
    a = (gb * y).astype(BF16)
    mix = (jnp.dot(a, wout_ref[0:cw, :], preferred_element_type=F32)
           + jnp.dot(attn_ref[...], wout_ref[cw:, :], preferred_element_type=F32))
    _mix_ln_route(mix, x_ref, lng_ref, lnb_ref, wr_ref, br_ref, xo_ref, rec_ref, alpha)


def _outproj_odd_kernel(attn_ref, x_ref, wout_ref, lng_ref, lnb_ref, wr_ref, br_ref,
                        xo_ref, rec_ref, *, alpha):
    mix = jnp.dot(attn_ref[...], wout_ref[...], preferred_element_type=F32)
    _mix_ln_route(mix, x_ref, lng_ref, lnb_ref, wr_ref, br_ref, xo_ref, rec_ref, alpha)


def _outproj(acts, xf, wout, convw, lng, lnb, wr, br, *, seq, alpha):
    n, d = xf.shape
    tm = min(ROW_TILE, seq)
    row_spec = lambda width: pl.BlockSpec((tm, width), lambda i: (i, 0))
    full = lambda a: pl.BlockSpec(a.shape, lambda i: (0, 0))
    even = convw is not None
    if even:
        body = functools.partial(_outproj_even_kernel, tiles_per_seq=seq // tm, alpha=alpha)
        operands = (*acts, xf, wout, convw, lng, lnb, wr, br)
        scratch = [pltpu.VMEM((8, acts[0].shape[1] // 3), F32)]
    else:
        body = functools.partial(_outproj_odd_kernel, alpha=alpha)
        operands = (*acts, xf, wout, lng, lnb, wr, br)
        scratch = []
    in_specs = [row_spec(a.shape[1]) for a in acts] + [row_spec(d)]
    in_specs += [full(a) for a in operands[len(acts) + 1:]]
    return pl.pallas_call(
        body,
        out_shape=(jax.ShapeDtypeStruct((n, d), F32), jax.ShapeDtypeStruct((n, LANES), F32)),
        grid=(n // tm,),
        in_specs=in_specs,
        out_specs=(row_spec(d), row_spec(LANES)),
        scratch_shapes=scratch,
        compiler_params=_params(1),
        name="outproj_even" if even else "outproj_odd",
    )(*operands)


def _dispatch_plan(rec, n_tok, rows):
    eid = rec[:, 0:2].astype(jnp.int32).reshape(-1)
    wt = rec[:, 2:4].reshape(-1)
    onehot = (eid[:, None] == jnp.arange(N_EXPERTS, dtype=jnp.int32)[None, :]).astype(jnp.int32)
    csum = jnp.cumsum(onehot, axis=0)
    rank = jnp.sum(csum * onehot, axis=1) - 1
    counts = csum[-1]
    padded = (counts + rows - 1) // rows * rows
    pad_end = jnp.cumsum(padded)
    pos = (pad_end - padded)[eid] + rank
    cap = 2 * n_tok + N_EXPERTS * rows
    n_blocks = cap // rows
    row_tok = jnp.zeros((cap,), jnp.int32).at[pos].set(jnp.arange(2 * n_tok, dtype=jnp.int32) // 2)
    row_w = jnp.zeros((cap,), F32).at[pos].set(wt)
    blk_start = jnp.arange(n_blocks, dtype=jnp.int32) * rows
    blk_e = jnp.minimum(jnp.sum((pad_end[None, :] <= blk_start[:, None]).astype(jnp.int32), axis=1),
                        N_EXPERTS - 1)
    n_used = (pad_end[-1] // rows).astype(jnp.int32).reshape(1)
    return pos.reshape(n_tok, 2), row_tok, row_w, blk_e, n_used


def _row_copy(src_hbm, dst_ref, src_row, dst_row, sem):
    return pltpu.make_async_copy(src_hbm.at[pl.ds(src_row, 1)], dst_ref.at[pl.ds(dst_row, 1)], sem)


def _gather_rows_kernel(n_used_ref, idx_ref, x_hbm, out_ref, sem):
    i = pl.program_id(0)
    rows = out_ref.shape[0]

    @pl.when(i < n_used_ref[0])
    def _():
        def issue(r, carry):
            _row_copy(x_hbm, out_ref, idx_ref[0, 0, r], r, sem).start()
            return carry

        lax.fori_loop(0, rows, issue, 0)
        pltpu.make_async_copy(x_hbm.at[pl.ds(0, rows)], out_ref, sem).wait()

    @pl.when(i >= n_used_ref[0])
    def _():
        out_ref[...] = jnp.zeros_like(out_ref)


def _gather_rows(xf, row_tok, n_used, rows):
    cap = row_tok.shape[0]
    n_blocks = cap // rows
    d = xf.shape[1]
    grid_spec = pltpu.PrefetchScalarGridSpec(
        num_scalar_prefetch=1,
        grid=(n_blocks,),
        in_specs=[pl.BlockSpec((1, 1, rows), lambda i, nu: (i, 0, 0), memory_space=pltpu.SMEM),
                  pl.BlockSpec(memory_space=pl.ANY)],
        out_specs=pl.BlockSpec((rows, d), lambda i, nu: (i, 0)),
        scratch_shapes=[pltpu.SemaphoreType.DMA(())],
    )
    return pl.pallas_call(
        _gather_rows_kernel,
        out_shape=jax.ShapeDtypeStruct((cap, d), xf.dtype),
        grid_spec=grid_spec,
        compiler_params=_params(1),
        name="moe_gather",
    )(n_used, row_tok.reshape(n_blocks, 1, rows), xf)


def _expert_mlp_kernel(blk_e_ref, n_used_ref, xb_ref, wg_ref, wu_ref, wd_ref, roww_ref, out_ref,
                       wgu_s, wd_s):
    i = pl.program_id(0)
    dh = wg_ref.shape[2]
    changed = jnp.logical_or(i == 0, blk_e_ref[i] != blk_e_ref[jnp.maximum(i - 1, 0)])

    @pl.when(changed)
    def _():
        wgu_s[:, 0:dh] = wg_ref[0].astype(BF16)
        wgu_s[:, dh:2 * dh] = wu_ref[0].astype(BF16)
        wd_s[...] = wd_ref[0].astype(BF16)

    @pl.when(i < n_used_ref[0])
    def _():
        x = xb_ref[...].astype(BF16)
        gu = jnp.dot(x, wgu_s[...], preferred_element_type=F32)
        g = gu[:, 0:dh]
        hid = (g / (1.0 + jnp.exp(-g))) * gu[:, dh:2 * dh]
        y = jnp.dot(hid.astype(BF16), wd_s[...], preferred_element_type=F32)
        out_ref[...] = y * roww_ref[...]

    @pl.when(i >= n_used_ref[0])
    def _():
        out_ref[...] = jnp.zeros_like(out_ref)


def _expert_mlp(xb, w_gate, w_up, w_down, row_w, blk_e, n_used, rows):
    cap, d = xb.shape
    dh = w_gate.shape[2]
    grid_spec = pltpu.PrefetchScalarGridSpec(
        num_scalar_prefetch=2,
        grid=(cap // rows,),
        in_specs=[pl.BlockSpec((rows, d), lambda i, be, nu: (i, 0)),
                  pl.BlockSpec((1, d, dh), lambda i, be, nu: (be[i], 0, 0)),
                  pl.BlockSpec((1, d, dh), lambda i, be, nu: (be[i], 0, 0)),
                  pl.BlockSpec((1, dh, d), lambda i, be, nu: (be[i], 0, 0)),
                  pl.BlockSpec((rows, 1), lambda i, be, nu: (i, 0))],
        out_specs=pl.BlockSpec((rows, d), lambda i, be, nu: (i, 0)),
        scratch_shapes=[pltpu.VMEM((d, 2 * dh), BF16), pltpu.VMEM((dh, d), BF16)],
    )
    return pl.pallas_call(
        _expert_mlp_kernel,
        out_shape=jax.ShapeDtypeStruct((cap, d), F32),
        grid_spec=grid_spec,
        compiler_params=_params(1),
        name="moe_experts",
    )(blk_e, n_used, xb, w_gate, w_up, w_down, row_w.reshape(cap, 1))


def _combine_ln_kernel(pos_ref, yb_hbm, x_ref, lng_ref, lnb_ref, xo_ref, ya_ref, yb_ref, sem,
                       *, alpha):
    tm = x_ref.shape[0]

    def issue(r, carry):
        _row_copy(yb_hbm, ya_ref, pos_ref[0, 0, r], r, sem).start()
        _row_copy(yb_hbm, yb_ref, pos_ref[0, 0, tm + r], r, sem).start()
        return carry

    lax.fori_loop(0, tm, issue, 0)
    pltpu.make_async_copy(yb_hbm.at[pl.ds(0, tm)], ya_ref, sem).wait()
    pltpu.make_async_copy(yb_hbm.at[pl.ds(0, tm)], yb_ref, sem).wait()
    ffn = ya_ref[...] + yb_ref[...]
    xo_ref[...] = _layer_norm(alpha * x_ref[...] + ffn, lng_ref[...], lnb_ref[...])


def _combine_ln(yb, pos, xf, lng, lnb, *, alpha):
    n, d = xf.shape
    tm = ROW_TILE
    n_tiles = n // tm
    pos_tiles = jnp.transpose(pos.reshape(n_tiles, tm, 2), (0, 2, 1)).reshape(n_tiles, 1, 2 * tm)
    return pl.pallas_call(
        functools.partial(_combine_ln_kernel, alpha=alpha),
        out_shape=jax.ShapeDtypeStruct((n, d), F32),
        grid=(n_tiles,),
        in_specs=[pl.BlockSpec((1, 1, 2 * tm), lambda i: (i, 0, 0), memory_space=pltpu.SMEM),
                  pl.BlockSpec(memory_space=pl.ANY),
                  pl.BlockSpec((tm, d), lambda i: (i, 0)),
                  pl.BlockSpec(lng.shape, lambda i: (0, 0)),
                  pl.BlockSpec(lnb.shape, lambda i: (0, 0))],
        out_specs=pl.BlockSpec((tm, d), lambda i: (i, 0)),
        scratch_shapes=[pltpu.VMEM((tm, d), F32), pltpu.VMEM((tm, d), F32),
                        pltpu.SemaphoreType.DMA(())],
        compiler_params=_params(1),
        name="moe_combine_ln",
    )(pos_tiles, yb, xf, lng, lnb)


def kernel(x, ab_w_in, ab_b_forget, ab_conv_w, ab_w_out, c_w_in, c_lam_q1, c_lam_k1, c_lam_q2,
           c_lam_k2, c_subln_g, c_w_out, ln_mix_g, ln_mix_b, ln_ffn_g, ln_ffn_b, moe_w_group,
           moe_b_group, moe_w_expert, moe_b_expert, moe_w_gate, moe_w_up, moe_w_down):
    batch, seq, d = x.shape
    depth = ln_mix_g.shape[0]
    n_tok = batch * seq
    alpha = (2.0 * depth) ** 0.25
    q_scale = HEAD_DIM ** -0.5 * LOG2E
    conv_ch = ab_conv_w.shape[1]
    fox_heads = ab_b_forget.shape[1]
    n_diff_heads = d // (2 * HEAD_DIM)
    slopes = jnp.asarray(2.0 ** (-8.0 * np.arange(1, n_diff_heads + 1) / n_diff_heads), F32)
    xf = x.reshape(n_tok, d)

    def pad_lanes(v):
        return jnp.pad(v, (0, LANES - v.shape[0]))

    for layer in range(depth):
        i = layer // 2
        w_router = jnp.pad(jnp.concatenate([moe_w_group[layer], moe_w_expert[layer]], axis=1),
                           ((0, 0), (0, LANES - N_GROUPS - N_EXPERTS)))
        b_router = pad_lanes(jnp.concatenate([moe_b_group[layer], moe_b_expert[layer]]))[None, :]
        lng, lnb = ln_mix_g[layer][None, :], ln_mix_b[layer][None, :]
        if layer % 2 == 0:
            w = ab_w_in[i]
            q_lo = 3 * conv_ch
            q_hi = q_lo + fox_heads * HEAD_DIM
            w = w.at[:, q_lo:q_hi].multiply(q_scale)
            n_main = w.shape[1] - fox_heads
            w_main = w[:, :n_main].astype(BF16)
            wf = jnp.pad(w[:, n_main:], ((0, 0), (0, LANES - fox_heads))).astype(BF16)
            gates, qkv, csplit = _inproj_even(xf, w_main, wf, pad_lanes(ab_b_forget[i])[None, :],
                                              seq=seq)
            attn = _fox_attention(qkv, csplit, batch=batch, seq=seq)
            convw = jnp.pad(jnp.transpose(ab_conv_w[i]), ((0, 8 - ab_conv_w.shape[2]), (0, 0)))
            xf, rec = _outproj((gates, attn), xf, ab_w_out[i].astype(BF16), convw, lng, lnb,
                               w_router, b_router, seq=seq, alpha=alpha)
        else:
            w = c_w_in[i].at[:, 0:d].multiply(q_scale)
            h = _inproj_odd(xf, w.astype(BF16))
            lam_init = 0.8 - 0.6 * math.exp(-0.3 * layer)
            lam_rows = jnp.pad(jnp.stack([pad_lanes(c_lam_q1[i]), pad_lanes(c_lam_k1[i]),
                                          pad_lanes(c_lam_q2[i]), pad_lanes(c_lam_k2[i])]),
                               ((0, 4), (0, 0)))
            attn = _diff_attention(h, slopes, lam_rows, c_subln_g[i][None, :],
                                   batch=batch, seq=seq, lam_init=lam_init)
            xf, rec = _outproj((attn,), xf, c_w_out[i].astype(BF16), None, lng, lnb,
                               w_router, b_router, seq=seq, alpha=alpha)

        pos, row_tok, row_w, blk_e, n_used = _dispatch_plan(rec, n_tok, MOE_ROWS)
        xb = _gather_rows(xf, row_tok, n_used, MOE_ROWS)
        yb = _expert_mlp(xb, moe_w_gate[layer], moe_w_up[layer], moe_w_down[layer], row_w,
                         blk_e, n_used, MOE_ROWS)
        xf = _combine_ln(yb, pos, xf, ln_ffn_g[layer][None, :], ln_ffn_b[layer][None, :],
                         alpha=alpha)
    return xf.reshape(batch, seq, d)
```

```python
import functools
import math

import numpy as np
import jax
import jax.numpy as jnp
from jax import lax
from jax.experimental import pallas as pl
from jax.experimental.pallas import tpu as pltpu

F32 = jnp.float32
BF16 = jnp.bfloat16

LANES = 128
HEAD_DIM = 64
N_GROUPS = 4
EXPERTS_PER_GROUP = 8
N_EXPERTS = N_GROUPS * EXPERTS_PER_GROUP
CHUNK = 64
LN_EPS = 1e-5
RMS_EPS = 1e-5
LOG2E = math.log2(math.e)
M_INIT = -1e30
MASKED = -3e38
VMEM_LIMIT = 48 * 1024 * 1024

ROW_TILE = 256
PROJ_TILE = 512
ATTN_TILE = 256
SOFTMAX_ROWS = 32
MOE_ROWS = 256
COL_CHUNK = 512

_NT = (((1,), (1,)), ((), ()))


def _params(n_grid):
    return pltpu.CompilerParams(dimension_semantics=("arbitrary",) * n_grid,
                                vmem_limit_bytes=VMEM_LIMIT)


def _layer_norm(z, g, b):
    mu = jnp.mean(z, axis=-1, keepdims=True)
    zc = z - mu
    var = jnp.mean(zc * zc, axis=-1, keepdims=True)
    return zc * lax.rsqrt(var + LN_EPS) * g + b


def _split3(v):
    a0 = v.astype(BF16)
    r1 = v - a0.astype(F32)
    a1 = r1.astype(BF16)
    a2 = (r1 - a1.astype(F32)).astype(BF16)
    return a0, a1, a2


def _inproj_even_kernel(x_ref, w_ref, wf_ref, bf_ref, gates_ref, qkv_ref, csplit_ref, carry_ref,
                        *, tiles_per_seq):
    i = pl.program_id(0)
    x = x_ref[...].astype(BF16)
    n_gate = gates_ref.shape[1]
    for j in range(n_gate // COL_CHUNK):
        sl = slice(j * COL_CHUNK, (j + 1) * COL_CHUNK)
        gates_ref[:, sl] = jnp.dot(x, w_ref[:, sl], preferred_element_type=F32)
    for j in range(qkv_ref.shape[1] // COL_CHUNK):
        sl = slice(j * COL_CHUNK, (j + 1) * COL_CHUNK)
        wsl = slice(n_gate + j * COL_CHUNK, n_gate + (j + 1) * COL_CHUNK)
        qkv_ref[:, sl] = jnp.dot(x, w_ref[:, wsl], preferred_element_type=F32).astype(BF16)

    z = jnp.dot(x, wf_ref[...], preferred_element_type=F32) + bf_ref[...]
    lf = jnp.minimum(z, 0.0) - jnp.log1p(jnp.exp(-jnp.abs(z)))
    tm = lf.shape[0]
    r = lax.broadcasted_iota(jnp.int32, (tm, tm), 0)
    c = lax.broadcasted_iota(jnp.int32, (tm, tm), 1)
    tri = jnp.where(c <= r, 1.0, 0.0).astype(BF16)
    a0, a1, a2 = _split3(lf)
    cs = (jnp.dot(tri, a0, preferred_element_type=F32)
          + jnp.dot(tri, a1, preferred_element_type=F32)
          + jnp.dot(tri, a2, preferred_element_type=F32))

    @pl.when(i % tiles_per_seq == 0)
    def _():
        carry_ref[...] = jnp.zeros_like(carry_ref)

    cs = cs + carry_ref[0:1, :]
    carry_ref[...] = jnp.broadcast_to(cs[tm - 1:tm, :], carry_ref.shape)
    c0, c1, c2 = _split3(cs * LOG2E)
    lane = lax.broadcasted_iota(jnp.int32, (tm, LANES), 1)
    csplit_ref[:, 0:LANES] = c0
    csplit_ref[:, LANES:2 * LANES] = c1
    csplit_ref[:, 2 * LANES:3 * LANES] = c2
    csplit_ref[:, 3 * LANES:4 * LANES] = jnp.where(lane == 0, 1.0, 0.0).astype(BF16)


def _inproj_even(xf, w_main, wf, bf, *, seq):
    n, d = xf.shape
    tm = min(PROJ_TILE, seq)
    n_gate = 3 * (d // 2)
    n_qkv = w_main.shape[1] - n_gate
    return pl.pallas_call(
        functools.partial(_inproj_even_kernel, tiles_per_seq=seq // tm),
        out_shape=(jax.ShapeDtypeStruct((n, n_gate), F32),
                   jax.ShapeDtypeStruct((n, n_qkv), BF16),
                   jax.ShapeDtypeStruct((n, 4 * LANES), BF16)),
        grid=(n // tm,),
        in_specs=[pl.BlockSpec((tm, d), lambda i: (i, 0)),
                  pl.BlockSpec(w_main.shape, lambda i: (0, 0)),
                  pl.BlockSpec(wf.shape, lambda i: (0, 0)),
                  pl.BlockSpec(bf.shape, lambda i: (0, 0))],
        out_specs=(pl.BlockSpec((tm, n_gate), lambda i: (i, 0)),
                   pl.BlockSpec((tm, n_qkv), lambda i: (i, 0)),
                   pl.BlockSpec((tm, 4 * LANES), lambda i: (i, 0))),
        scratch_shapes=[pltpu.VMEM((8, LANES), F32)],
        compiler_params=_params(1),
        name="inproj_even",
    )(xf, w_main, wf, bf)


def _inproj_odd_kernel(x_ref, w_ref, h_ref):
    x = x_ref[...].astype(BF16)
    for j in range(h_ref.shape[1] // COL_CHUNK):
        sl = slice(j * COL_CHUNK, (j + 1) * COL_CHUNK)
        h_ref[:, sl] = jnp.dot(x, w_ref[:, sl], preferred_element_type=F32).astype(BF16)


def _inproj_odd(xf, w):
    n, d = xf.shape
    tm = PROJ_TILE
    return pl.pallas_call(
        _inproj_odd_kernel,
        out_shape=jax.ShapeDtypeStruct((n, w.shape[1]), BF16),
        grid=(n // tm,),
        in_specs=[pl.BlockSpec((tm, d), lambda i: (i, 0)),
                  pl.BlockSpec(w.shape, lambda i: (0, 0))],
        out_specs=pl.BlockSpec((tm, w.shape[1]), lambda i: (i, 0)),
        compiler_params=_params(1),
        name="inproj_odd",
    )(xf, w)


def _flash_scratch(tq, seq):
    r2 = 2 * tq
    return ([pltpu.VMEM((seq, 2 * LANES), BF16)]
            + [pltpu.VMEM((r2, tq), F32) for _ in range(2)]
            + [pltpu.VMEM((r2, tq), BF16) for _ in range(2)]
            + [pltpu.VMEM((r2, LANES), F32) for _ in range(2)]
            + [pltpu.VMEM((r2, LANES), F32) for _ in range(3)])


def _stack_queries(q, feat_a, feat_b):
    lane = lax.broadcasted_iota(jnp.int32, q.shape, 1)
    zero = jnp.zeros_like(q)
    qa = jnp.concatenate([jnp.where(lane < HEAD_DIM, q, zero), feat_a], axis=1)
    qb = jnp.concatenate([jnp.where(lane >= HEAD_DIM, q, zero), feat_b], axis=1)
    return jnp.concatenate([qa, qb], axis=0)


def _flash_pipeline(qaug, kaug_ref, v_ref, scratch, qi, tq, diag_bias_ref):
    s_x, s_y, p_x, p_y, al_x, al_y, m_scr, l_scr, acc_scr = scratch
    r2 = 2 * tq
    n_rep = tq // LANES
    m_scr[...] = jnp.full(m_scr.shape, M_INIT, F32)
    l_scr[...] = jnp.zeros(l_scr.shape, F32)
    acc_scr[...] = jnp.zeros(acc_scr.shape, F32)
    p_y[...] = jnp.zeros(p_y.shape, BF16)
    al_y[...] = jnp.ones(al_y.shape, F32)

    def scores(j, s_ref):
        off = pl.multiple_of(j * tq, tq)
        s_ref[...] = lax.dot_general(qaug, kaug_ref[pl.ds(off, tq), :], _NT,
                                     preferred_element_type=F32)

    def weighted_values(j, p_ref, al_ref):
        off = pl.multiple_of(j * tq, tq)
        acc_scr[...] = al_ref[...] * acc_scr[...] + jnp.dot(
            p_ref[...], v_ref[pl.ds(off, tq), :], preferred_element_type=F32)

    def softmax(s_ref, p_ref, al_ref, bias_fn):
        for c in range(r2 // SOFTMAX_ROWS):
            rows = slice(c * SOFTMAX_ROWS, (c + 1) * SOFTMAX_ROWS)
            x = s_ref[rows, :]
            if bias_fn is not None:
                x = x + bias_fn(c)
            m_old = m_scr[rows, :]
            m_new = jnp.maximum(m_old, jnp.max(x, axis=1, keepdims=True))
            alpha = jnp.exp2(m_old - m_new)
            p = jnp.exp2(x - jnp.concatenate([m_new] * n_rep, axis=1))
            p_sum = p[:, 0:LANES]
            for t in range(1, n_rep):
                p_sum = p_sum + p[:, t * LANES:(t + 1) * LANES]
            m_scr[rows, :] = m_new
            l_scr[rows, :] = alpha * l_scr[rows, :] + p_sum
            al_ref[rows, :] = alpha
            p_ref[rows, :] = p.astype(BF16)

    scores(0, s_x)

    def pair(i, carry):
        e = 2 * i
        weighted_values(jnp.maximum(e - 1, 0), p_y, al_y)
        scores(e + 1, s_y)
        softmax(s_x, p_x, al_x, None)
        weighted_values(e, p_x, al_x)
        scores(e + 2, s_x)
        softmax(s_y, p_y, al_y, None)
        return carry

    lax.fori_loop(0, qi // 2, pair, 0)

    e = 2 * (qi // 2)
    gate = jnp.where(qi % 2 == 1, 0.0, MASKED).astype(F32)
    weighted_values(jnp.maximum(e - 1, 0), p_y, al_y)
    scores(qi, s_y)
    softmax(s_x, p_x, al_x, lambda c: gate)
    weighted_values(e, p_x, al_x)

    def diag_bias(c):
        r0 = (c * SOFTMAX_ROWS) % tq
        return diag_bias_ref[r0:r0 + SOFTMAX_ROWS, :]

    softmax(s_y, p_y, al_y, diag_bias)
    weighted_values(qi, p_y, al_y)
    return jnp.sum(l_scr[...], axis=1, keepdims=True), acc_scr[...]


def _fox_kernel(q_ref, k_ref, v_ref, cs_ref, eq_ref, ek_ref, o_ref, diag_ref, kaug_ref, *scratch,
                tq):
    qi = pl.program_id(2)

    @pl.when(qi == 0)
    def _():
        kaug_ref[:, 0:LANES] = k_ref[...]
        kaug_ref[:, LANES:] = jnp.dot(cs_ref[...], ek_ref[0],
                                      preferred_element_type=F32).astype(BF16)

    row = lax.broadcasted_iota(jnp.int32, (tq, tq), 0)
    col = lax.broadcasted_iota(jnp.int32, (tq, tq), 1)
    diag_ref[...] = jnp.where(col <= row, 0.0, MASKED)
    cs_q = cs_ref[pl.ds(pl.multiple_of(qi * tq, tq), tq), :]
    feats = [jnp.dot(cs_q, eq_ref[0, sub], preferred_element_type=F32).astype(BF16)
             for sub in range(2)]
    qaug = _stack_queries(q_ref[...], feats[0], feats[1])
    l, acc = _flash_pipeline(qaug, kaug_ref, v_ref, scratch, qi, tq, diag_ref)
    out = acc / l
    lane = lax.broadcasted_iota(jnp.int32, (tq, LANES), 1)
    o_ref[...] = jnp.where(lane < HEAD_DIM, out[0:tq], out[tq:2 * tq]).astype(o_ref.dtype)


def _fox_feature_maps(n_pairs):
    one_row = 3 * LANES
    ek = np.zeros((n_pairs, 4 * LANES, LANES), np.float32)
    eq = np.zeros((n_pairs, 2, 4 * LANES, LANES), np.float32)
    for p in range(n_pairs):
        for sub in range(2):
            head, base = 2 * p + sub, 6 * sub
            for t in range(3):
                ek[p, one_row, base + t] = 1.0
                ek[p, t * LANES + head, base + 3 + t] = -1.0
                eq[p, sub, t * LANES + head, base + t] = 1.0
                eq[p, sub, one_row, base + 3 + t] = 1.0
    return jnp.asarray(eq, BF16), jnp.asarray(ek, BF16)


def _fox_attention(qkv, csplit, *, batch, seq):
    n, width3 = qkv.shape
    n_pairs = width3 // 3 // LANES
    tq = min(ATTN_TILE, seq)
    nq = seq // tq
    eq, ek = _fox_feature_maps(n_pairs)
    return pl.pallas_call(
        functools.partial(_fox_kernel, tq=tq),
        out_shape=jax.ShapeDtypeStruct((n, n_pairs * LANES), BF16),
        grid=(batch, n_pairs, nq),
        in_specs=[pl.BlockSpec((tq, LANES), lambda b, h, i: (b * nq + i, h)),
                  pl.BlockSpec((seq, LANES), lambda b, h, i: (b, n_pairs + h)),
                  pl.BlockSpec((seq, LANES), lambda b, h, i: (b, 2 * n_pairs + h)),
                  pl.BlockSpec((seq, csplit.shape[1]), lambda b, h, i: (b, 0)),
                  pl.BlockSpec((1,) + eq.shape[1:], lambda b, h, i: (h, 0, 0, 0)),
                  pl.BlockSpec((1,) + ek.shape[1:], lambda b, h, i: (h, 0, 0))],
        out_specs=pl.BlockSpec((tq, LANES), lambda b, h, i: (b * nq + i, h)),
        scratch_shapes=[pltpu.VMEM((tq, tq), F32)] + _flash_scratch(tq, seq),
        compiler_params=_params(3),
        name="fox_attention",
    )(qkv, qkv, qkv, csplit, eq, ek)


def _diff_kernel(slopes_ref, q_ref, k_ref, v_ref, lam_ref, g_ref, o_ref, diag_ref, kaug_ref,
                 *scratch, tq, lam_init):
    h = pl.program_id(1)
    qi = pl.program_id(2)
    seq = k_ref.shape[0]
    slope2 = slopes_ref[h] * LOG2E

    def bias_lanes(n_rows, first, base, other, other_value):
        lane = lax.broadcasted_iota(jnp.int32, (n_rows, LANES), 1)
        pos = (lax.broadcasted_iota(jnp.int32, (n_rows, LANES), 0) + first).astype(F32) * slope2
        terms = [t.astype(F32) for t in _split3(pos)]
        feat = jnp.where((lane >= other) & (lane < other + 3), other_value, 0.0)
        for t, term in enumerate(terms):
            feat = jnp.where(lane == base + t, term, feat)
        return feat.astype(BF16)

    @pl.when(qi == 0)
    def _():
        kaug_ref[:, 0:LANES] = k_ref[...]
        kaug_ref[:, LANES:] = bias_lanes(seq, 0, 0, 3, -1.0)

    qfeat = bias_lanes(tq, qi * tq, 3, 0, 1.0)
    row = lax.broadcasted_iota(jnp.int32, (tq, tq), 0)
    col = lax.broadcasted_iota(jnp.int32, (tq, tq), 1)
    ahead = jnp.maximum(col - row, 0).astype(F32)
    diag_ref[...] = jnp.where((col // CHUNK) <= (row // CHUNK), -2.0 * slope2 * ahead, MASKED)
    qaug = _stack_queries(q_ref[...], qfeat, qfeat)
    l, acc = _flash_pipeline(qaug, kaug_ref, v_ref, scratch, qi, tq, diag_ref)
    out = acc / l
    t1_ = jnp.sum(lam_ref[0:1, :] * lam_ref[1:2, :], axis=-1, keepdims=True)
    t2_ = jnp.sum(lam_ref[2:3, :] * lam_ref[3:4, :], axis=-1, keepdims=True)
    lam = jnp.exp(t1_) - jnp.exp(t2_) + lam_init
    o = out[0:tq] - lam * out[tq:2 * tq]
    o = o * lax.rsqrt(jnp.mean(o * o, axis=-1, keepdims=True) + RMS_EPS)
    o_ref[...] = (o * g_ref[...] * (1.0 - lam_init)).astype(o_ref.dtype)


def _diff_attention(h, slopes, lam_rows, subln_g, *, batch, seq, lam_init):
    n, width3 = h.shape
    n_heads = width3 // 3 // LANES
    tq = min(ATTN_TILE, seq)
    nq = seq // tq
    grid_spec = pltpu.PrefetchScalarGridSpec(
        num_scalar_prefetch=1,
        grid=(batch, n_heads, nq),
        in_specs=[pl.BlockSpec((tq, LANES), lambda b, hd, i, s: (b * nq + i, hd)),
                  pl.BlockSpec((seq, LANES), lambda b, hd, i, s: (b, n_heads + hd)),
                  pl.BlockSpec((seq, LANES), lambda b, hd, i, s: (b, 2 * n_heads + hd)),
                  pl.BlockSpec(lam_rows.shape, lambda b, hd, i, s: (0, 0)),
                  pl.BlockSpec(subln_g.shape, lambda b, hd, i, s: (0, 0))],
        out_specs=pl.BlockSpec((tq, LANES), lambda b, hd, i, s: (b * nq + i, hd)),
        scratch_shapes=[pltpu.VMEM((tq, tq), F32)] + _flash_scratch(tq, seq),
    )
    return pl.pallas_call(
        functools.partial(_diff_kernel, tq=tq, lam_init=lam_init),
        out_shape=jax.ShapeDtypeStruct((n, n_heads * LANES), BF16),
        grid_spec=grid_spec,
        compiler_params=_params(3),
        name="diff_attention",
    )(slopes, h, h, h, lam_rows, subln_g)


def _route(lg):
    lane = lax.broadcasted_iota(jnp.int32, lg.shape, 1).astype(F32)
    low = jnp.float32(-3e38)
    is_g = lane < N_GROUPS
    gmax = jnp.max(jnp.where(is_g, lg, low), axis=-1, keepdims=True)
    gsel = jnp.min(jnp.where(is_g & (lg == gmax), lane, float(LANES)), axis=-1, keepdims=True)
    gsum = jnp.sum(jnp.where(is_g, jnp.exp(lg - gmax), 0.0), axis=-1, keepdims=True)
    gw = 1.0 / gsum
    lo = N_GROUPS + EXPERTS_PER_GROUP * gsel
    in_grp = (lane >= lo) & (lane < lo + EXPERTS_PER_GROUP)
    el = jnp.where(in_grp, lg, low)
    v1 = jnp.max(el, axis=-1, keepdims=True)
    i1 = jnp.min(jnp.where(in_grp & (lg == v1), lane, float(LANES)), axis=-1, keepdims=True)
    rest = in_grp & (lane != i1)
    v2 = jnp.max(jnp.where(rest, lg, low), axis=-1, keepdims=True)
    i2 = jnp.min(jnp.where(rest & (lg == v2), lane, float(LANES)), axis=-1, keepdims=True)
    t = jnp.exp(v2 - v1)
    w1 = gw / (1.0 + t)
    w2 = gw * (t / (1.0 + t))
    rec = jnp.where(lane == 0.0, i1 - N_GROUPS,
                    jnp.where(lane == 1.0, i2 - N_GROUPS,
                              jnp.where(lane == 2.0, w1, jnp.where(lane == 3.0, w2, 0.0))))
    return rec


def _mix_ln_route(mix, x_ref, lng_ref, lnb_ref, wr_ref, br_ref, xo_ref, rec_ref, alpha):
    xo = _layer_norm(alpha * x_ref[...] + mix, lng_ref[...], lnb_ref[...])
    xo_ref[...] = xo
    lg = jnp.dot(xo, wr_ref[...], preferred_element_type=F32,
                 precision=lax.Precision.HIGHEST) + br_ref[...]
    rec_ref[...] = _route(lg)


def _outproj_even_kernel(gates_ref, attn_ref, x_ref, wout_ref, convw_ref, lng_ref, lnb_ref,
                         wr_ref, br_ref, xo_ref, rec_ref, tail_ref, *, tiles_per_seq, alpha):
    i = pl.program_id(0)
    tm = x_ref.shape[0]
    cw = gates_ref.shape[1] // 3
    gb = gates_ref[:, 0:cw]
    u = gates_ref[:, cw:2 * cw] * gates_ref[:, 2 * cw:3 * cw]

    @pl.when(i % tiles_per_seq == 0)
    def _():
        tail_ref[...] = jnp.zeros_like(tail_ref)

    tail = tail_ref[...]
    row = lax.broadcasted_iota(jnp.int32, (tm, 1), 0)
    u1 = jnp.where(row == 0, tail[7:8, :], pltpu.roll(u, 1, axis=0))
    u2 = jnp.where(row == 0, tail[6:7, :],
                   jnp.where(row == 1, tail[7:8, :], pltpu.roll(u, 2, axis=0)))
    tail_ref[...] = u[tm - 8:tm, :]
    y = u2 * convw_ref[0:1, :] + u1 * convw_ref[1:2, :] + u * convw_ref[2:3, :]
    a = (gb * y).astype(BF16)
    mix = (jnp.dot(a, wout_ref[0:cw, :], preferred_element_type=F32)
           + jnp.dot(attn_ref[...], wout_ref[cw:, :], preferred_element_type=F32))
    _mix_ln_route(mix, x_ref, lng_ref, lnb_ref, wr_ref, br_ref, xo_ref, rec_ref, alpha)


def _outproj_odd_kernel(attn_ref, x_ref, wout_ref, lng_ref, lnb_ref, wr_ref, br_ref,
                        xo_ref, rec_ref, *, alpha):
    mix = jnp.dot(attn_ref[...], wout_ref[...], preferred_element_type=F32)
    _mix_ln_route(mix, x_ref, lng_ref, lnb_ref, wr_ref, br_ref, xo_ref, rec_ref, alpha)


def _outproj(acts, xf, wout, convw, lng, lnb, wr, br, *, seq, alpha):
    n, d = xf.shape
    tm = min(ROW_TILE, seq)
    row_spec = lambda width: pl.BlockSpec((tm, width), lambda i: (i, 0))
    full = lambda a: pl.BlockSpec(a.shape, lambda i: (0, 0))
    even = convw is not None
    if even:
        body = functools.partial(_outproj_even_kernel, tiles_per_seq=seq // tm, alpha=alpha)
        operands = (*acts, xf, wout, convw, lng, lnb, wr, br)
        scratch = [pltpu.VMEM((8, acts[0].shape[1] // 3), F32)]
    else:
        body = functools.partial(_outproj_odd_kernel, alpha=alpha)
        operands = (*acts, xf, wout, lng, lnb, wr, br)
        scratch = []
    in_specs = [row_spec(a.shape[1]) for a in acts] + [row_spec(d)]
    in_specs += [full(a) for a in operands[len(acts) + 1:]]
    return pl.pallas_call(
        body,
        out_shape=(jax.ShapeDtypeStruct((n, d), F32), jax.ShapeDtypeStruct((n, LANES), F32)),
        grid=(n // tm,),
        in_specs=in_specs,
        out_specs=(row_spec(d), row_spec(LANES)),
        scratch_shapes=scratch,
        compiler_params=_params(1),
        name="outproj_even" if even else "outproj_odd",
    )(*operands)


def _dispatch_plan(rec, n_tok, rows):
    eid = rec[:, 0:2].astype(jnp.int32).reshape(-1)
    onehot = (eid[:, None] == jnp.arange(N_EXPERTS, dtype=jnp.int32)[None, :]).astype(jnp.int32)
    csum = jnp.cumsum(onehot, axis=0)
    rank = jnp.sum(csum * onehot, axis=1) - 1
    counts = csum[-1]
    padded = (counts + rows - 1) // rows * rows
    pad_end = jnp.cumsum(padded)
    pos = (pad_end - padded)[eid] + rank
    cap = 2 * n_tok + N_EXPERTS * rows
    n_blocks = cap // rows
    n_pairs = 2 * n_tok
    row_pair = jnp.full((cap,), -1, jnp.int32).at[pos].set(jnp.arange(n_pairs, dtype=jnp.int32))
    src = jnp.where(row_pair >= 0, row_pair >> 1, 0)
    src_ext = jnp.concatenate([src, jnp.zeros((rows,), jnp.int32)]).reshape(n_blocks + 1, 1, rows)
    tm = ROW_TILE
    n_tiles = n_tok // tm
    pos_ext = jnp.transpose(pos.reshape(n_tiles, tm, 2), (0, 2, 1)).reshape(n_tiles, 2 * tm)
    pos_ext = jnp.concatenate([pos_ext, jnp.zeros((1, 2 * tm), jnp.int32)])
    pos_ext = pos_ext.reshape(n_tiles + 1, 1, 2 * tm)
    blk_start = jnp.arange(n_blocks, dtype=jnp.int32) * rows
    blk_e = jnp.minimum(jnp.sum((pad_end[None, :] <= blk_start[:, None]).astype(jnp.int32), axis=1),
                        N_EXPERTS - 1)
    return src_ext, pos_ext, blk_e


def _start_rows(src_hbm, idx_ref, idx_base, dst_ref, slot, sem, n):
    for r in range(n):
        pltpu.make_async_copy(src_hbm.at[pl.ds(idx_ref[0, 0, idx_base + r], 1)],
                              dst_ref.at[slot, pl.ds(r, 1)], sem.at[slot]).start()


def _wait_rows(src_hbm, dst_ref, slot, sem):
    n = dst_ref.shape[1]
    pltpu.make_async_copy(src_hbm.at[pl.ds(0, n)], dst_ref.at[slot], sem.at[slot]).wait()


def _expert_mlp_kernel(blk_e_ref, src0_ref, src1_ref, x_hbm, wg_ref, wu_ref, wd_ref, out_ref,
                       xbuf, wgu_s, wd_s, sem):
    i = pl.program_id(0)
    rows = xbuf.shape[1]
    dh = wg_ref.shape[2]
    slot = i % 2

    @pl.when(i == 0)
    def _():
        _start_rows(x_hbm, src0_ref, 0, xbuf, 0, sem, rows)

    _wait_rows(x_hbm, xbuf, slot, sem)
    changed = jnp.logical_or(i == 0, blk_e_ref[i] != blk_e_ref[jnp.maximum(i - 1, 0)])

    @pl.when(changed)
    def _():
        wgu_s[:, 0:dh] = wg_ref[0].astype(BF16)
        wgu_s[:, dh:2 * dh] = wu_ref[0].astype(BF16)
        wd_s[...] = wd_ref[0].astype(BF16)

    x = xbuf[slot].astype(BF16)
    _start_rows(x_hbm, src1_ref, 0, xbuf, 1 - slot, sem, rows)
    gu = jnp.dot(x, wgu_s[...], preferred_element_type=F32)
    g = gu[:, 0:dh]
    hid = (g / (1.0 + jnp.exp(-g))) * gu[:, dh:2 * dh]
    out_ref[...] = jnp.dot(hid.astype(BF16), wd_s[...], preferred_element_type=F32)

    @pl.when(i == pl.num_programs(0) - 1)
    def _():
        _wait_rows(x_hbm, xbuf, 1 - slot, sem)


def _expert_mlp(xf, w_gate, w_up, w_down, src_ext, blk_e, rows):
    n_tok, d = xf.shape
    dh = w_gate.shape[2]
    n_blocks = blk_e.shape[0]
    idx_spec = lambda shift: pl.BlockSpec((1, 1, rows), lambda i, be: (i + shift, 0, 0),
                                          memory_space=pltpu.SMEM)
    grid_spec = pltpu.PrefetchScalarGridSpec(
        num_scalar_prefetch=1,
        grid=(n_blocks,),
        in_specs=[idx_spec(0), idx_spec(1),
                  pl.BlockSpec(memory_space=pl.ANY),
                  pl.BlockSpec((1, d, dh), lambda i, be: (be[i], 0, 0)),
                  pl.BlockSpec((1, d, dh), lambda i, be: (be[i], 0, 0)),
                  pl.BlockSpec((1, dh, d), lambda i, be: (be[i], 0, 0))],
        out_specs=pl.BlockSpec((rows, d), lambda i, be: (i, 0)),
        scratch_shapes=[pltpu.VMEM((2, rows, d), F32),
                        pltpu.VMEM((d, 2 * dh), BF16), pltpu.VMEM((dh, d), BF16),
                        pltpu.SemaphoreType.DMA((2,))],
    )
    return pl.pallas_call(
        _expert_mlp_kernel,
        out_shape=jax.ShapeDtypeStruct((n_blocks * rows, d), F32),
        grid_spec=grid_spec,
        compiler_params=_params(1),
        name="moe_experts",
    )(blk_e, src_ext, src_ext, xf, w_gate, w_up, w_down)


def _combine_ln_kernel(pos0_ref, pos1_ref, yb_hbm, rec_ref, x_ref, lng_ref, lnb_ref, xo_ref,
                       ya_buf, yb_buf, sem_a, sem_b, *, alpha):
    i = pl.program_id(0)
    tm = x_ref.shape[0]
    slot = i % 2

    @pl.when(i == 0)
    def _():
        _start_rows(yb_hbm, pos0_ref, 0, ya_buf, 0, sem_a, tm)
        _start_rows(yb_hbm, pos0_ref, tm, yb_buf, 0, sem_b, tm)

    _wait_rows(yb_hbm, ya_buf, slot, sem_a)
    _wait_rows(yb_hbm, yb_buf, slot, sem_b)
    z = alpha * x_ref[...] + (rec_ref[:, 2:3] * ya_buf[slot] + rec_ref[:, 3:4] * yb_buf[slot])
    _start_rows(yb_hbm, pos1_ref, 0, ya_buf, 1 - slot, sem_a, tm)
    _start_rows(yb_hbm, pos1_ref, tm, yb_buf, 1 - slot, sem_b, tm)
    xo_ref[...] = _layer_norm(z, lng_ref[...], lnb_ref[...])

    @pl.when(i == pl.num_programs(0) - 1)
    def _():
        _wait_rows(yb_hbm, ya_buf, 1 - slot, sem_a)
        _wait_rows(yb_hbm, yb_buf, 1 - slot, sem_b)


def _combine_ln(yb, pos_ext, rec, xf, lng, lnb, *, alpha):
    n, d = xf.shape
    tm = ROW_TILE
    idx_spec = lambda shift: pl.BlockSpec((1, 1, 2 * tm), lambda i: (i + shift, 0, 0),
                                          memory_space=pltpu.SMEM)
    return pl.pallas_call(
        functools.partial(_combine_ln_kernel, alpha=alpha),
        out_shape=jax.ShapeDtypeStruct((n, d), F32),
        grid=(n // tm,),
        in_specs=[idx_spec(0), idx_spec(1),
                  pl.BlockSpec(memory_space=pl.ANY),
                  pl.BlockSpec((tm, LANES), lambda i: (i, 0)),
                  pl.BlockSpec((tm, d), lambda i: (i, 0)),
                  pl.BlockSpec(lng.shape, lambda i: (0, 0)),
                  pl.BlockSpec(lnb.shape, lambda i: (0, 0))],
        out_specs=pl.BlockSpec((tm, d), lambda i: (i, 0)),
        scratch_shapes=[pltpu.VMEM((2, tm, d), F32), pltpu.VMEM((2, tm, d), F32),
                        pltpu.SemaphoreType.DMA((2,)), pltpu.SemaphoreType.DMA((2,))],
        compiler_params=_params(1),
        name="moe_combine_ln",
    )(pos_ext, pos_ext, yb, rec, xf, lng, lnb)


def kernel(x, ab_w_in, ab_b_forget, ab_conv_w, ab_w_out, c_w_in, c_lam_q1, c_lam_k1, c_lam_q2,
           c_lam_k2, c_subln_g, c_w_out, ln_mix_g, ln_mix_b, ln_ffn_g, ln_ffn_b, moe_w_group,
           moe_b_group, moe_w_expert, moe_b_expert, moe_w_gate, moe_w_up, moe_w_down):
    batch, seq, d = x.shape
    depth = ln_mix_g.shape[0]
    n_tok = batch * seq
    alpha = (2.0 * depth) ** 0.25
    q_scale = HEAD_DIM ** -0.5 * LOG2E
    conv_ch = ab_conv_w.shape[1]
    fox_heads = ab_b_forget.shape[1]
    n_diff_heads = d // (2 * HEAD_DIM)
    slopes = jnp.asarray(2.0 ** (-8.0 * np.arange(1, n_diff_heads + 1) / n_diff_heads), F32)
    xf = x.reshape(n_tok, d)

    def pad_lanes(v):
        return jnp.pad(v, (0, LANES - v.shape[0]))

    for layer in range(depth):
        i = layer // 2
        w_router = jnp.pad(jnp.concatenate([moe_w_group[layer], moe_w_expert[layer]], axis=1),
                           ((0, 0), (0, LANES - N_GROUPS - N_EXPERTS)))
        b_router = pad_lanes(jnp.concatenate([moe_b_group[layer], moe_b_expert[layer]]))[None, :]
        lng, lnb = ln_mix_g[layer][None, :], ln_mix_b[layer][None, :]
        if layer % 2 == 0:
            w = ab_w_in[i]
            q_lo = 3 * conv_ch
            q_hi = q_lo + fox_heads * HEAD_DIM
            w = w.at[:, q_lo:q_hi].multiply(q_scale)
            n_main = w.shape[1] - fox_heads
            w_main = w[:, :n_main].astype(BF16)
            wf = jnp.pad(w[:, n_main:], ((0, 0), (0, LANES - fox_heads))).astype(BF16)
            gates, qkv, csplit = _inproj_even(xf, w_main, wf, pad_lanes(ab_b_forget[i])[None, :],
                                              seq=seq)
            attn = _fox_attention(qkv, csplit, batch=batch, seq=seq)
            convw = jnp.pad(jnp.transpose(ab_conv_w[i]), ((0, 8 - ab_conv_w.shape[2]), (0, 0)))
            xf, rec = _outproj((gates, attn), xf, ab_w_out[i].astype(BF16), convw, lng, lnb,
                               w_router, b_router, seq=seq, alpha=alpha)
        else:
            w = c_w_in[i].at[:, 0:d].multiply(q_scale)
            h = _inproj_odd(xf, w.astype(BF16))
            lam_init = 0.8 - 0.6 * math.exp(-0.3 * layer)
            lam_rows = jnp.pad(jnp.stack([pad_lanes(c_lam_q1[i]), pad_lanes(c_lam_k1[i]),
                                          pad_lanes(c_lam_q2[i]), pad_lanes(c_lam_k2[i])]),
                               ((0, 4), (0, 0)))
            attn = _diff_attention(h, slopes, lam_rows, c_subln_g[i][None, :],
                                   batch=batch, seq=seq, lam_init=lam_init)
            xf, rec = _outproj((attn,), xf, c_w_out[i].astype(BF16), None, lng, lnb,
                               w_router, b_router, seq=seq, alpha=alpha)

        src_ext, pos_ext, blk_e = _dispatch_plan(rec, n_tok, MOE_ROWS)
        yb = _expert_mlp(xf, moe_w_gate[layer], moe_w_up[layer], moe_w_down[layer], src_ext,
                         blk_e, MOE_ROWS)
        xf = _combine_ln(yb, pos_ext, rec, xf, ln_ffn_g[layer][None, :],
                         ln_ffn_b[layer][None, :], alpha=alpha)
    return xf.reshape(batch, seq, d)
```

```python
import functools
import math

import numpy as np
import jax
import jax.numpy as jnp
from jax import lax
from jax.experimental import pallas as pl
from jax.experimental.pallas import tpu as pltpu

F32 = jnp.float32
BF16 = jnp.bfloat16

LANES = 128
HEAD_DIM = 64
N_GROUPS = 4
EXPERTS_PER_GROUP = 8
N_EXPERTS = N_GROUPS * EXPERTS_PER_GROUP
CHUNK = 64
LN_EPS = 1e-5
RMS_EPS = 1e-5
LOG2E = math.log2(math.e)
M_INIT = -1e30
MASKED = -3e38
VMEM_LIMIT = 48 * 1024 * 1024

ROW_TILE = 256
PROJ_TILE = 512
ATTN_TILE = 256
SOFTMAX_ROWS = 32
MOE_ROWS = 256
COL_CHUNK = 512

_NT = (((1,), (1,)), ((), ()))


def _params(n_grid):
    return pltpu.CompilerParams(dimension_semantics=("arbitrary",) * n_grid,
                                vmem_limit_bytes=VMEM_LIMIT)


def _layer_norm(z, g, b):
    mu = jnp.mean(z, axis=-1, keepdims=True)
    zc = z - mu
    var = jnp.mean(zc * zc, axis=-1, keepdims=True)
    return zc * lax.rsqrt(var + LN_EPS) * g + b


def _split3(v):
    a0 = v.astype(BF16)
    r1 = v - a0.astype(F32)
    a1 = r1.astype(BF16)
    a2 = (r1 - a1.astype(F32)).astype(BF16)
    return a0, a1, a2


def _inproj_even_kernel(x_ref, w_ref, wf_ref, bf_ref, gates_ref, qkv_ref, csplit_ref, carry_ref,
                        *, tiles_per_seq):
    i = pl.program_id(0)
    x = x_ref[...].astype(BF16)
    n_gate = gates_ref.shape[1]
    for j in range(n_gate // COL_CHUNK):
        sl = slice(j * COL_CHUNK, (j + 1) * COL_CHUNK)
        gates_ref[:, sl] = jnp.dot(x, w_ref[:, sl], preferred_element_type=F32)
    for j in range(qkv_ref.shape[1] // COL_CHUNK):
        sl = slice(j * COL_CHUNK, (j + 1) * COL_CHUNK)
        wsl = slice(n_gate + j * COL_CHUNK, n_gate + (j + 1) * COL_CHUNK)
        qkv_ref[:, sl] = jnp.dot(x, w_ref[:, wsl], preferred_element_type=F32).astype(BF16)

    z = jnp.dot(x, wf_ref[...], preferred_element_type=F32) + bf_ref[...]
    lf = jnp.minimum(z, 0.0) - jnp.log1p(jnp.exp(-jnp.abs(z)))
    tm = lf.shape[0]
    r = lax.broadcasted_iota(jnp.int32, (tm, tm), 0)
    c = lax.broadcasted_iota(jnp.int32, (tm, tm), 1)
    tri = jnp.where(c <= r, 1.0, 0.0).astype(BF16)
    a0, a1, a2 = _split3(lf)
    cs = (jnp.dot(tri, a0, preferred_element_type=F32)
          + jnp.dot(tri, a1, preferred_element_type=F32)
          + jnp.dot(tri, a2, preferred_element_type=F32))

    @pl.when(i % tiles_per_seq == 0)
    def _():
        carry_ref[...] = jnp.zeros_like(carry_ref)

    cs = cs + carry_ref[0:1, :]
    carry_ref[...] = jnp.broadcast_to(cs[tm - 1:tm, :], carry_ref.shape)
    c0, c1, c2 = _split3(cs * LOG2E)
    lane = lax.broadcasted_iota(jnp.int32, (tm, LANES), 1)
    csplit_ref[:, 0:LANES] = c0
    csplit_ref[:, LANES:2 * LANES] = c1
    csplit_ref[:, 2 * LANES:3 * LANES] = c2
    csplit_ref[:, 3 * LANES:4 * LANES] = jnp.where(lane == 0, 1.0, 0.0).astype(BF16)


def _inproj_even(xf, w_main, wf, bf, *, seq):
    n, d = xf.shape
    tm = min(PROJ_TILE, seq)
    n_gate = 3 * (d // 2)
    n_qkv = w_main.shape[1] - n_gate
    return pl.pallas_call(
        functools.partial(_inproj_even_kernel, tiles_per_seq=seq // tm),
        out_shape=(jax.ShapeDtypeStruct((n, n_gate), F32),
                   jax.ShapeDtypeStruct((n, n_qkv), BF16),
                   jax.ShapeDtypeStruct((n, 4 * LANES), BF16)),
        grid=(n // tm,),
        in_specs=[pl.BlockSpec((tm, d), lambda i: (i, 0)),
                  pl.BlockSpec(w_main.shape, lambda i: (0, 0)),
                  pl.BlockSpec(wf.shape, lambda i: (0, 0)),
                  pl.BlockSpec(bf.shape, lambda i: (0, 0))],
        out_specs=(pl.BlockSpec((tm, n_gate), lambda i: (i, 0)),
                   pl.BlockSpec((tm, n_qkv), lambda i: (i, 0)),
                   pl.BlockSpec((tm, 4 * LANES), lambda i: (i, 0))),
        scratch_shapes=[pltpu.VMEM((8, LANES), F32)],
        compiler_params=_params(1),
        name="inproj_even",
    )(xf, w_main, wf, bf)


def _inproj_odd_kernel(x_ref, w_ref, h_ref):
    x = x_ref[...].astype(BF16)
    for j in range(h_ref.shape[1] // COL_CHUNK):
        sl = slice(j * COL_CHUNK, (j + 1) * COL_CHUNK)
        h_ref[:, sl] = jnp.dot(x, w_ref[:, sl], preferred_element_type=F32).astype(BF16)


def _inproj_odd(xf, w):
    n, d = xf.shape
    tm = PROJ_TILE
    return pl.pallas_call(
        _inproj_odd_kernel,
        out_shape=jax.ShapeDtypeStruct((n, w.shape[1]), BF16),
        grid=(n // tm,),
        in_specs=[pl.BlockSpec((tm, d), lambda i: (i, 0)),
                  pl.BlockSpec(w.shape, lambda i: (0, 0))],
        out_specs=pl.BlockSpec((tm, w.shape[1]), lambda i: (i, 0)),
        compiler_params=_params(1),
        name="inproj_odd",
    )(xf, w)


def _flash_scratch(tq, seq):
    r2 = 2 * tq
    return ([pltpu.VMEM((seq, 2 * LANES), BF16)]
            + [pltpu.VMEM((r2, tq), F32) for _ in range(2)]
            + [pltpu.VMEM((r2, tq), BF16) for _ in range(2)]
            + [pltpu.VMEM((r2, LANES), F32) for _ in range(2)]
            + [pltpu.VMEM((r2, LANES), F32) for _ in range(3)])


def _stack_queries(q, feat_a, feat_b):
    lane = lax.broadcasted_iota(jnp.int32, q.shape, 1)
    zero = jnp.zeros_like(q)
    qa = jnp.concatenate([jnp.where(lane < HEAD_DIM, q, zero), feat_a], axis=1)
    qb = jnp.concatenate([jnp.where(lane >= HEAD_DIM, q, zero), feat_b], axis=1)
    return jnp.concatenate([qa, qb], axis=0)


def _flash_pipeline(qaug, kaug_ref, v_ref, scratch, qi, tq, diag_bias_ref):
    s_x, s_y, p_x, p_y, al_x, al_y, m_scr, l_scr, acc_scr = scratch
    r2 = 2 * tq
    n_rep = tq // LANES

    def scores(j, s_ref):
        off = pl.multiple_of(j * tq, tq)
        s_ref[...] = lax.dot_general(qaug, kaug_ref[pl.ds(off, tq), :], _NT,
                                     preferred_element_type=F32)

    def weighted_values(j, p_ref, al_ref):
        off = pl.multiple_of(j * tq, tq)
        acc_scr[...] = al_ref[...] * acc_scr[...] + jnp.dot(
            p_ref[...], v_ref[pl.ds(off, tq), :], preferred_element_type=F32)

    def softmax(s_ref, p_ref, al_ref, bias_fn):
        for c in range(r2 // SOFTMAX_ROWS):
            rows = slice(c * SOFTMAX_ROWS, (c + 1) * SOFTMAX_ROWS)
            x = s_ref[rows, :]
            if bias_fn is not None:
                x = x + bias_fn(c)
            m_old = m_scr[rows, :]
            m_new = jnp.maximum(m_old, jnp.max(x, axis=1, keepdims=True))
            alpha = jnp.exp2(m_old - m_new)
            p = jnp.exp2(x - jnp.concatenate([m_new] * n_rep, axis=1))
            p_sum = p[:, 0:LANES]
            for t in range(1, n_rep):
                p_sum = p_sum + p[:, t * LANES:(t + 1) * LANES]
            m_scr[rows, :] = m_new
            l_scr[rows, :] = alpha * l_scr[rows, :] + p_sum
            al_ref[rows, :] = alpha
            p_ref[rows, :] = p.astype(BF16)

    def diag_bias(c):
        r0 = (c * SOFTMAX_ROWS) % tq
        return diag_bias_ref[r0:r0 + SOFTMAX_ROWS, :]

    scores(0, s_x)
    m_scr[...] = jnp.full(m_scr.shape, M_INIT, F32)
    l_scr[...] = jnp.zeros(l_scr.shape, F32)
    acc_scr[...] = jnp.zeros(acc_scr.shape, F32)
    p_y[...] = jnp.zeros(p_y.shape, BF16)
    al_y[...] = jnp.ones(al_y.shape, F32)

    def pair(i, carry):
        e = 2 * i
        weighted_values(jnp.maximum(e - 1, 0), p_y, al_y)
        scores(e + 1, s_y)
        softmax(s_x, p_x, al_x, None)
        weighted_values(e, p_x, al_x)
        scores(e + 2, s_x)
        softmax(s_y, p_y, al_y, None)
        return carry

    lax.fori_loop(0, qi // 2, pair, 0)
    e = 2 * (qi // 2)
    weighted_values(jnp.maximum(e - 1, 0), p_y, al_y)

    @pl.when(qi % 2 == 1)
    def _():
        scores(qi, s_y)
        softmax(s_x, p_x, al_x, None)
        weighted_values(e, p_x, al_x)
        softmax(s_y, p_y, al_y, diag_bias)
        weighted_values(qi, p_y, al_y)

    @pl.when(qi % 2 == 0)
    def _():
        softmax(s_x, p_x, al_x, diag_bias)
        weighted_values(qi, p_x, al_x)

    return jnp.sum(l_scr[...], axis=1, keepdims=True), acc_scr[...]


def _fox_kernel(q_ref, k_ref, v_ref, cs_ref, eq_ref, ek_ref, o_ref, diag_ref, kaug_ref, *scratch,
                tq):
    qi = pl.program_id(2)

    @pl.when(qi == 0)
    def _():
        kaug_ref[:, 0:LANES] = k_ref[...]
        kaug_ref[:, LANES:] = jnp.dot(cs_ref[...], ek_ref[0],
                                      preferred_element_type=F32).astype(BF16)

    row = lax.broadcasted_iota(jnp.int32, (tq, tq), 0)
    col = lax.broadcasted_iota(jnp.int32, (tq, tq), 1)
    diag_ref[...] = jnp.where(col <= row, 0.0, MASKED)
    cs_q = cs_ref[pl.ds(pl.multiple_of(qi * tq, tq), tq), :]
    feats = [jnp.dot(cs_q, eq_ref[0, sub], preferred_element_type=F32).astype(BF16)
             for sub in range(2)]
    qaug = _stack_queries(q_ref[...], feats[0], feats[1])
    l, acc = _flash_pipeline(qaug, kaug_ref, v_ref, scratch, qi, tq, diag_ref)
    out = acc / l
    lane = lax.broadcasted_iota(jnp.int32, (tq, LANES), 1)
    o_ref[...] = jnp.where(lane < HEAD_DIM, out[0:tq], out[tq:2 * tq]).astype(o_ref.dtype)


def _fox_feature_maps(n_pairs):
    one_row = 3 * LANES
    ek = np.zeros((n_pairs, 4 * LANES, LANES), np.float32)
    eq = np.zeros((n_pairs, 2, 4 * LANES, LANES), np.float32)
    for p in range(n_pairs):
        for sub in range(2):
            head, base = 2 * p + sub, 6 * sub
            for t in range(3):
                ek[p, one_row, base + t] = 1.0
                ek[p, t * LANES + head, base + 3 + t] = -1.0
                eq[p, sub, t * LANES + head, base + t] = 1.0
                eq[p, sub, one_row, base + 3 + t] = 1.0
    return jnp.asarray(eq, BF16), jnp.asarray(ek, BF16)


def _fox_attention(qkv, csplit, *, batch, seq):
    n, width3 = qkv.shape
    n_pairs = width3 // 3 // LANES
    tq = min(ATTN_TILE, seq)
    nq = seq // tq
    eq, ek = _fox_feature_maps(n_pairs)
    return pl.pallas_call(
        functools.partial(_fox_kernel, tq=tq),
        out_shape=jax.ShapeDtypeStruct((n, n_pairs * LANES), BF16),
        grid=(batch, n_pairs, nq),
        in_specs=[pl.BlockSpec((tq, LANES), lambda b, h, i: (b * nq + i, h)),
                  pl.BlockSpec((seq, LANES), lambda b, h, i: (b, n_pairs + h)),
                  pl.BlockSpec((seq, LANES), lambda b, h, i: (b, 2 * n_pairs + h)),
                  pl.BlockSpec((seq, csplit.shape[1]), lambda b, h, i: (b, 0)),
                  pl.BlockSpec((1,) + eq.shape[1:], lambda b, h, i: (h, 0, 0, 0)),
                  pl.BlockSpec((1,) + ek.shape[1:], lambda b, h, i: (h, 0, 0))],
        out_specs=pl.BlockSpec((tq, LANES), lambda b, h, i: (b * nq + i, h)),
        scratch_shapes=[pltpu.VMEM((tq, tq), F32)] + _flash_scratch(tq, seq),
        compiler_params=_params(3),
        name="fox_attention",
    )(qkv, qkv, qkv, csplit, eq, ek)


def _diff_kernel(slopes_ref, q_ref, k_ref, v_ref, lam_ref, g_ref, o_ref, diag_ref, kaug_ref,
                 *scratch, tq, lam_init):
    h = pl.program_id(1)
    qi = pl.program_id(2)
    seq = k_ref.shape[0]
    slope2 = slopes_ref[h] * LOG2E

    def bias_lanes(n_rows, first, base, other, other_value):
        lane = lax.broadcasted_iota(jnp.int32, (n_rows, LANES), 1)
        pos = (lax.broadcasted_iota(jnp.int32, (n_rows, LANES), 0) + first).astype(F32) * slope2
        terms = [t.astype(F32) for t in _split3(pos)]
        feat = jnp.where((lane >= other) & (lane < other + 3), other_value, 0.0)
        for t, term in enumerate(terms):
            feat = jnp.where(lane == base + t, term, feat)
        return feat.astype(BF16)

    @pl.when(qi == 0)
    def _():
        kaug_ref[:, 0:LANES] = k_ref[...]
        kaug_ref[:, LANES:] = bias_lanes(seq, 0, 0, 3, -1.0)

    qfeat = bias_lanes(tq, qi * tq, 3, 0, 1.0)
    row = lax.broadcasted_iota(jnp.int32, (tq, tq), 0)
    col = lax.broadcasted_iota(jnp.int32, (tq, tq), 1)
    ahead = jnp.maximum(col - row, 0).astype(F32)
    diag_ref[...] = jnp.where((col // CHUNK) <= (row // CHUNK), -2.0 * slope2 * ahead, MASKED)
    qaug = _stack_queries(q_ref[...], qfeat, qfeat)
    l, acc = _flash_pipeline(qaug, kaug_ref, v_ref, scratch, qi, tq, diag_ref)
    out = acc / l
    t1_ = jnp.sum(lam_ref[0:1, :] * lam_ref[1:2, :], axis=-1, keepdims=True)
    t2_ = jnp.sum(lam_ref[2:3, :] * lam_ref[3:4, :], axis=-1, keepdims=True)
    lam = jnp.exp(t1_) - jnp.exp(t2_) + lam_init
    o = out[0:tq] - lam * out[tq:2 * tq]
    o = o * lax.rsqrt(jnp.mean(o * o, axis=-1, keepdims=True) + RMS_EPS)
    o_ref[...] = (o * g_ref[...] * (1.0 - lam_init)).astype(o_ref.dtype)


def _diff_attention(h, slopes, lam_rows, subln_g, *, batch, seq, lam_init):
    n, width3 = h.shape
    n_heads = width3 // 3 // LANES
    tq = min(ATTN_TILE, seq)
    nq = seq // tq
    grid_spec = pltpu.PrefetchScalarGridSpec(
        num_scalar_prefetch=1,
        grid=(batch, n_heads, nq),
        in_specs=[pl.BlockSpec((tq, LANES), lambda b, hd, i, s: (b * nq + i, hd)),
                  pl.BlockSpec((seq, LANES), lambda b, hd, i, s: (b, n_heads + hd)),
                  pl.BlockSpec((seq, LANES), lambda b, hd, i, s: (b, 2 * n_heads + hd)),
                  pl.BlockSpec(lam_rows.shape, lambda b, hd, i, s: (0, 0)),
                  pl.BlockSpec(subln_g.shape, lambda b, hd, i, s: (0, 0))],
        out_specs=pl.BlockSpec((tq, LANES), lambda b, hd, i, s: (b * nq + i, hd)),
        scratch_shapes=[pltpu.VMEM((tq, tq), F32)] + _flash_scratch(tq, seq),
    )
    return pl.pallas_call(
        functools.partial(_diff_kernel, tq=tq, lam_init=lam_init),
        out_shape=jax.ShapeDtypeStruct((n, n_heads * LANES), BF16),
        grid_spec=grid_spec,
        compiler_params=_params(3),
        name="diff_attention",
    )(slopes, h, h, h, lam_rows, subln_g)


def _route(lg):
    lane = lax.broadcasted_iota(jnp.int32, lg.shape, 1).astype(F32)
    low = jnp.float32(-3e38)
    is_g = lane < N_GROUPS
    gmax = jnp.max(jnp.where(is_g, lg, low), axis=-1, keepdims=True)
    gsel = jnp.min(jnp.where(is_g & (lg == gmax), lane, float(LANES)), axis=-1, keepdims=True)
    gsum = jnp.sum(jnp.where(is_g, jnp.exp(lg - gmax), 0.0), axis=-1, keepdims=True)
    gw = 1.0 / gsum
    lo = N_GROUPS + EXPERTS_PER_GROUP * gsel
    in_grp = (lane >= lo) & (lane < lo + EXPERTS_PER_GROUP)
    el = jnp.where(in_grp, lg, low)
    v1 = jnp.max(el, axis=-1, keepdims=True)
    i1 = jnp.min(jnp.where(in_grp & (lg == v1), lane, float(LANES)), axis=-1, keepdims=True)
    rest = in_grp & (lane != i1)
    v2 = jnp.max(jnp.where(rest, lg, low), axis=-1, keepdims=True)
    i2 = jnp.min(jnp.where(rest & (lg == v2), lane, float(LANES)), axis=-1, keepdims=True)
    t = jnp.exp(v2 - v1)
    w1 = gw / (1.0 + t)
    w2 = gw * (t / (1.0 + t))
    rec = jnp.where(lane == 0.0, i1 - N_GROUPS,
                    jnp.where(lane == 1.0, i2 - N_GROUPS,
                              jnp.where(lane == 2.0, w1, jnp.where(lane == 3.0, w2, 0.0))))
    return rec


def _mix_ln_route(mix, x_ref, lng_ref, lnb_ref, wr_ref, br_ref, xo_ref, rec_ref, alpha):
    xo = _layer_norm(alpha * x_ref[...] + mix, lng_ref[...], lnb_ref[...])
    xo_ref[...] = xo
    lg = jnp.dot(xo, wr_ref[...], preferred_element_type=F32,
                 precision=lax.Precision.HIGHEST) + br_ref[...]
    rec_ref[...] = _route(lg)


def _outproj_even_kernel(gates_ref, attn_ref, x_ref, wout_ref, convw_ref, lng_ref, lnb_ref,
                         wr_ref, br_ref, xo_ref, rec_ref, tail_ref, *, tiles_per_seq, alpha):
    i = pl.program_id(0)
    tm = x_ref.shape[0]
    cw = gates_ref.shape[1] // 3
    gb = gates_ref[:, 0:cw]
    u = gates_ref[:, cw:2 * cw] * gates_ref[:, 2 * cw:3 * cw]

    @pl.when(i % tiles_per_seq == 0)
    def _():
        tail_ref[...] = jnp.zeros_like(tail_ref)

    tail = tail_ref[...]
    row = lax.broadcasted_iota(jnp.int32, (tm, 1), 0)
    u1 = jnp.where(row == 0, tail[7:8, :], pltpu.roll(u, 1, axis=0))
    u2 = jnp.where(row == 0, tail[6:7, :],
                   jnp.where(row == 1, tail[7:8, :], pltpu.roll(u, 2, axis=0)))
    tail_ref[...] = u[tm - 8:tm, :]
    y = u2 * convw_ref[0:1, :] + u1 * convw_ref[1:2, :] + u * convw_ref[2:3, :]
    a = (gb * y).astype(BF16)
    mix = (jnp.dot(a, wout_ref[0:cw, :], preferred_element_type=F32)
           + jnp.dot(attn_ref[...], wout_ref[cw:, :], preferred_element_type=F32))
    _mix_ln_route(mix, x_ref, lng_ref, lnb_ref, wr_ref, br_ref, xo_ref, rec_ref, alpha)


def _outproj_odd_kernel(attn_ref, x_ref, wout_ref, lng_ref, lnb_ref, wr_ref, br_ref,
                        xo_ref, rec_ref, *, alpha):
    mix = jnp.dot(attn_ref[...], wout_ref[...], preferred_element_type=F32)
    _mix_ln_route(mix, x_ref, lng_ref, lnb_ref, wr_ref, br_ref, xo_ref, rec_ref, alpha)


def _outproj(acts, xf, wout, convw, lng, lnb, wr, br, *, seq, alpha):
    n, d = xf.shape
    tm = min(ROW_TILE, seq)
    row_spec = lambda width: pl.BlockSpec((tm, width), lambda i: (i, 0))
    full = lambda a: pl.BlockSpec(a.shape, lambda i: (0, 0))
    even = convw is not None
    if even:
        body = functools.partial(_outproj_even_kernel, tiles_per_seq=seq // tm, alpha=alpha)
        operands = (*acts, xf, wout, convw, lng, lnb, wr, br)
        scratch = [pltpu.VMEM((8, acts[0].shape[1] // 3), F32)]
    else:
        body = functools.partial(_outproj_odd_kernel, alpha=alpha)
        operands = (*acts, xf, wout, lng, lnb, wr, br)
        scratch = []
    in_specs = [row_spec(a.shape[1]) for a in acts] + [row_spec(d)]
    in_specs += [full(a) for a in operands[len(acts) + 1:]]
    return pl.pallas_call(
        body,
        out_shape=(jax.ShapeDtypeStruct((n, d), F32), jax.ShapeDtypeStruct((n, LANES), F32)),
        grid=(n // tm,),
        in_specs=in_specs,
        out_specs=(row_spec(d), row_spec(LANES)),
        scratch_shapes=scratch,
        compiler_params=_params(1),
        name="outproj_even" if even else "outproj_odd",
    )(*operands)


def _plan_kernel(rec_ref, start_ref, pos_ref, run_ref):
    i = pl.program_id(0)

    @pl.when(i == 0)
    def _():
        run_ref[...] = jnp.zeros_like(run_ref)

    rec = rec_ref[...]
    tm = rec.shape[0]
    lane = lax.broadcasted_iota(jnp.int32, rec.shape, 1).astype(F32)
    oh1 = lane == rec[:, 0:1]
    oh2 = lane == rec[:, 1:2]
    oh = jnp.where(oh1 | oh2, 1.0, 0.0)
    r = lax.broadcasted_iota(jnp.int32, (tm, tm), 0)
    c = lax.broadcasted_iota(jnp.int32, (tm, tm), 1)
    earlier = jnp.where(c < r, 1.0, 0.0).astype(BF16)
    base = (jnp.dot(earlier, oh.astype(BF16), preferred_element_type=F32)
            + run_ref[0:1, :] + start_ref[...])
    p1 = jnp.sum(jnp.where(oh1, base, 0.0), axis=1, keepdims=True)
    p2 = jnp.sum(jnp.where(oh2, base, 0.0), axis=1, keepdims=True)
    run_ref[...] = run_ref[...] + jnp.sum(oh, axis=0, keepdims=True)
    pos_ref[...] = jnp.where(lane == 0.0, p1, jnp.where(lane == 1.0, p2, 0.0)).astype(jnp.int32)


def _plan_rows(rec, seg_start):
    n = rec.shape[0]
    tm = ROW_TILE
    return pl.pallas_call(
        _plan_kernel,
        out_shape=jax.ShapeDtypeStruct((n, LANES), jnp.int32),
        grid=(n // tm,),
        in_specs=[pl.BlockSpec((tm, LANES), lambda i: (i, 0)),
                  pl.BlockSpec(seg_start.shape, lambda i: (0, 0))],
        out_specs=pl.BlockSpec((tm, LANES), lambda i: (i, 0)),
        scratch_shapes=[pltpu.VMEM((8, LANES), F32)],
        compiler_params=_params(1),
        name="moe_plan",
    )(rec, seg_start)


def _dispatch_plan(rec, n_tok, rows):
    eid = rec[:, 0:2].astype(jnp.int32).reshape(-1)
    counts = jnp.sum((eid[:, None] == jnp.arange(N_EXPERTS, dtype=jnp.int32)[None, :])
                     .astype(jnp.int32), axis=0)
    padded = (counts + rows - 1) // rows * rows
    pad_end = jnp.cumsum(padded)
    seg_start = jnp.pad((pad_end - padded).astype(F32), (0, LANES - N_EXPERTS))[None, :]
    pos = _plan_rows(rec, seg_start)[:, 0:2].reshape(-1)
    cap = 2 * n_tok + N_EXPERTS * rows
    n_blocks = cap // rows
    n_pairs = 2 * n_tok
    row_pair = jnp.full((cap,), -1, jnp.int32).at[pos].set(jnp.arange(n_pairs, dtype=jnp.int32))
    src = jnp.where(row_pair >= 0, row_pair >> 1, 0)
    src_ext = jnp.concatenate([src, jnp.zeros((rows,), jnp.int32)]).reshape(n_blocks + 1, 1, rows)
    tm = ROW_TILE
    n_tiles = n_tok // tm
    pos_ext = jnp.transpose(pos.reshape(n_tiles, tm, 2), (0, 2, 1)).reshape(n_tiles, 2 * tm)
    pos_ext = jnp.concatenate([pos_ext, jnp.zeros((1, 2 * tm), jnp.int32)])
    pos_ext = pos_ext.reshape(n_tiles + 1, 1, 2 * tm)
    blk_start = jnp.arange(n_blocks, dtype=jnp.int32) * rows
    blk_e = jnp.minimum(jnp.sum((pad_end[None, :] <= blk_start[:, None]).astype(jnp.int32), axis=1),
                        N_EXPERTS - 1)
    return src_ext, pos_ext, blk_e


def _start_rows(src_hbm, idx_ref, idx_base, dst_ref, slot, sem, n):
    for r in range(n):
        pltpu.make_async_copy(src_hbm.at[pl.ds(idx_ref[0, 0, idx_base + r], 1)],
                              dst_ref.at[slot, pl.ds(r, 1)], sem.at[slot]).start(priority=r % 2)


def _wait_rows(src_hbm, dst_ref, slot, sem):
    n = dst_ref.shape[1]
    pltpu.make_async_copy(src_hbm.at[pl.ds(0, n)], dst_ref.at[slot], sem.at[slot]).wait()


def _expert_mlp_kernel(blk_e_ref, src0_ref, src1_ref, x_hbm, wg_ref, wu_ref, wd_ref, out_ref,
                       xbuf, wgu_s, wd_s, sem):
    i = pl.program_id(0)
    rows = xbuf.shape[1]
    dh = wg_ref.shape[3]
    slot = i % 2

    @pl.when(i == 0)
    def _():
        _start_rows(x_hbm, src0_ref, 0, xbuf, 0, sem, rows)

    _wait_rows(x_hbm, xbuf, slot, sem)
    changed = jnp.logical_or(i == 0, blk_e_ref[i] != blk_e_ref[jnp.maximum(i - 1, 0)])

    @pl.when(changed)
    def _():
        wgu_s[:, 0:dh] = wg_ref[0, 0].astype(BF16)
        wgu_s[:, dh:2 * dh] = wu_ref[0, 0].astype(BF16)
        wd_s[...] = wd_ref[0, 0].astype(BF16)

    x = xbuf[slot].astype(BF16)
    _start_rows(x_hbm, src1_ref, 0, xbuf, 1 - slot, sem, rows)
    gu = jnp.dot(x, wgu_s[...], preferred_element_type=F32)
    g = gu[:, 0:dh]
    hid = (g / (1.0 + jnp.exp(-g))) * gu[:, dh:2 * dh]
    out_ref[...] = jnp.dot(hid.astype(BF16), wd_s[...], preferred_element_type=F32)

    @pl.when(i == pl.num_programs(0) - 1)
    def _():
        _wait_rows(x_hbm, xbuf, 1 - slot, sem)


def _expert_mlp(xf, w_gate, w_up, w_down, layer, src_ext, blk_e, rows):
    n_tok, d = xf.shape
    dh = w_gate.shape[3]
    n_blocks = blk_e.shape[0]
    idx_spec = lambda shift: pl.BlockSpec((1, 1, rows), lambda i, be: (i + shift, 0, 0),
                                          memory_space=pltpu.SMEM)
    grid_spec = pltpu.PrefetchScalarGridSpec(
        num_scalar_prefetch=1,
        grid=(n_blocks,),
        in_specs=[idx_spec(0), idx_spec(1),
                  pl.BlockSpec(memory_space=pl.ANY),
                  pl.BlockSpec((1, 1, d, dh), lambda i, be: (layer, be[i], 0, 0)),
                  pl.BlockSpec((1, 1, d, dh), lambda i, be: (layer, be[i], 0, 0)),
                  pl.BlockSpec((1, 1, dh, d), lambda i, be: (layer, be[i], 0, 0))],
        out_specs=pl.BlockSpec((rows, d), lambda i, be: (i, 0)),
        scratch_shapes=[pltpu.VMEM((2, rows, d), F32),
                        pltpu.VMEM((d, 2 * dh), BF16), pltpu.VMEM((dh, d), BF16),
                        pltpu.SemaphoreType.DMA((2,))],
    )
    return pl.pallas_call(
        _expert_mlp_kernel,
        out_shape=jax.ShapeDtypeStruct((n_blocks * rows, d), F32),
        grid_spec=grid_spec,
        compiler_params=_params(1),
        name="moe_experts",
    )(blk_e, src_ext, src_ext, xf, w_gate, w_up, w_down)


def _combine_ln_kernel(pos0_ref, pos1_ref, yb_hbm, rec_ref, x_ref, lng_ref, lnb_ref, xo_ref,
                       ya_buf, yb_buf, sem_a, sem_b, *, alpha):
    i = pl.program_id(0)
    tm = x_ref.shape[0]
    slot = i % 2

    @pl.when(i == 0)
    def _():
        _start_rows(yb_hbm, pos0_ref, 0, ya_buf, 0, sem_a, tm)
        _start_rows(yb_hbm, pos0_ref, tm, yb_buf, 0, sem_b, tm)

    _wait_rows(yb_hbm, ya_buf, slot, sem_a)
    _wait_rows(yb_hbm, yb_buf, slot, sem_b)
    z = alpha * x_ref[...] + (rec_ref[:, 2:3] * ya_buf[slot] + rec_ref[:, 3:4] * yb_buf[slot])
    _start_rows(yb_hbm, pos1_ref, 0, ya_buf, 1 - slot, sem_a, tm)
    _start_rows(yb_hbm, pos1_ref, tm, yb_buf, 1 - slot, sem_b, tm)
    xo_ref[...] = _layer_norm(z, lng_ref[...], lnb_ref[...])

    @pl.when(i == pl.num_programs(0) - 1)
    def _():
        _wait_rows(yb_hbm, ya_buf, 1 - slot, sem_a)
        _wait_rows(yb_hbm, yb_buf, 1 - slot, sem_b)


def _combine_ln(yb, pos_ext, rec, xf, lng, lnb, *, alpha):
    n, d = xf.shape
    tm = ROW_TILE
    idx_spec = lambda shift: pl.BlockSpec((1, 1, 2 * tm), lambda i: (i + shift, 0, 0),
                                          memory_space=pltpu.SMEM)
    return pl.pallas_call(
        functools.partial(_combine_ln_kernel, alpha=alpha),
        out_shape=jax.ShapeDtypeStruct((n, d), F32),
        grid=(n // tm,),
        in_specs=[idx_spec(0), idx_spec(1),
                  pl.BlockSpec(memory_space=pl.ANY),
                  pl.BlockSpec((tm, LANES), lambda i: (i, 0)),
                  pl.BlockSpec((tm, d), lambda i: (i, 0)),
                  pl.BlockSpec(lng.shape, lambda i: (0, 0)),
                  pl.BlockSpec(lnb.shape, lambda i: (0, 0))],
        out_specs=pl.BlockSpec((tm, d), lambda i: (i, 0)),
        scratch_shapes=[pltpu.VMEM((2, tm, d), F32), pltpu.VMEM((2, tm, d), F32),
                        pltpu.SemaphoreType.DMA((2,)), pltpu.SemaphoreType.DMA((2,))],
        compiler_params=_params(1),
        name="moe_combine_ln",
    )(pos_ext, pos_ext, yb, rec, xf, lng, lnb)


def kernel(x, ab_w_in, ab_b_forget, ab_conv_w, ab_w_out, c_w_in, c_lam_q1, c_lam_k1, c_lam_q2,
           c_lam_k2, c_subln_g, c_w_out, ln_mix_g, ln_mix_b, ln_ffn_g, ln_ffn_b, moe_w_group,
           moe_b_group, moe_w_expert, moe_b_expert, moe_w_gate, moe_w_up, moe_w_down):
    batch, seq, d = x.shape
    depth = ln_mix_g.shape[0]
    n_tok = batch * seq
    alpha = (2.0 * depth) ** 0.25
    q_scale = HEAD_DIM ** -0.5 * LOG2E
    conv_ch = ab_conv_w.shape[1]
    fox_heads = ab_b_forget.shape[1]
    n_diff_heads = d // (2 * HEAD_DIM)
    slopes = jnp.asarray(2.0 ** (-8.0 * np.arange(1, n_diff_heads + 1) / n_diff_heads), F32)
    xf = x.reshape(n_tok, d)

    def pad_lanes(v):
        return jnp.pad(v, (0, LANES - v.shape[0]))

    for layer in range(depth):
        i = layer // 2
        w_router = jnp.pad(jnp.concatenate([moe_w_group[layer], moe_w_expert[layer]], axis=1),
                           ((0, 0), (0, LANES - N_GROUPS - N_EXPERTS)))
        b_router = pad_lanes(jnp.concatenate([moe_b_group[layer], moe_b_expert[layer]]))[None, :]
        lng, lnb = ln_mix_g[layer][None, :], ln_mix_b[layer][None, :]
        if layer % 2 == 0:
            n_main = ab_w_in.shape[2] - fox_heads
            q_lo = 3 * conv_ch
            col_scale = np.ones((n_main,), np.float32)
            col_scale[q_lo:q_lo + fox_heads * HEAD_DIM] = q_scale
            w_main = (ab_w_in[i, :, :n_main] * col_scale).astype(BF16)
            wf = jnp.pad(ab_w_in[i, :, n_main:], ((0, 0), (0, LANES - fox_heads))).astype(BF16)
            gates, qkv, csplit = _inproj_even(xf, w_main, wf, pad_lanes(ab_b_forget[i])[None, :],
                                              seq=seq)
            attn = _fox_attention(qkv, csplit, batch=batch, seq=seq)
            convw = jnp.pad(jnp.transpose(ab_conv_w[i]), ((0, 8 - ab_conv_w.shape[2]), (0, 0)))
            xf, rec = _outproj((gates, attn), xf, ab_w_out[i].astype(BF16), convw, lng, lnb,
                               w_router, b_router, seq=seq, alpha=alpha)
        else:
            col_scale = np.ones((c_w_in.shape[2],), np.float32)
            col_scale[0:d] = q_scale
            h = _inproj_odd(xf, (c_w_in[i] * col_scale).astype(BF16))
            lam_init = 0.8 - 0.6 * math.exp(-0.3 * layer)
            lam_rows = jnp.pad(jnp.stack([pad_lanes(c_lam_q1[i]), pad_lanes(c_lam_k1[i]),
                                          pad_lanes(c_lam_q2[i]), pad_lanes(c_lam_k2[i])]),
                               ((0, 4), (0, 0)))
            attn = _diff_attention(h, slopes, lam_rows, c_subln_g[i][None, :],
                                   batch=batch, seq=seq, lam_init=lam_init)
            xf, rec = _outproj((attn,), xf, c_w_out[i].astype(BF16), None, lng, lnb,
                               w_router, b_router, seq=seq, alpha=alpha)

        src_ext, pos_ext, blk_e = _dispatch_plan(rec, n_tok, MOE_ROWS)
        yb = _expert_mlp(xf, moe_w_gate, moe_w_up, moe_w_down, layer, src_ext, blk_e, MOE_ROWS)
        xf = _combine_ln(yb, pos_ext, rec, xf, ln_ffn_g[layer][None, :],
                         ln_ffn_b[layer][None, :], alpha=alpha)
    return xf.reshape(batch, seq, d)
```

```python
import functools
import math

import numpy as np
import jax
import jax.numpy as jnp
from jax import lax
from jax.experimental import pallas as pl
from jax.experimental.pallas import tpu as pltpu

F32 = jnp.float32
BF16 = jnp.bfloat16

LANES = 128
HEAD_DIM = 64
N_GROUPS = 4
EXPERTS_PER_GROUP = 8
N_EXPERTS = N_GROUPS * EXPERTS_PER_GROUP
CHUNK = 64
LN_EPS = 1e-5
RMS_EPS = 1e-5
LOG2E = math.log2(math.e)
M_INIT = -1e30
MASKED = -3e38
VMEM_LIMIT = 48 * 1024 * 1024

ROW_TILE = 256
PROJ_TILE = 512
ATTN_TILE = 256
SOFTMAX_ROWS = 32
MOE_ROWS = 256
GATHER_BUFFERS = 3
COL_CHUNK = 512

_NT = (((1,), (1,)), ((), ()))


def _params(n_grid):
    return pltpu.CompilerParams(dimension_semantics=("arbitrary",) * n_grid,
                                vmem_limit_bytes=VMEM_LIMIT)


def _layer_norm(z, g, b):
    mu = jnp.mean(z, axis=-1, keepdims=True)
    zc = z - mu
    var = jnp.mean(zc * zc, axis=-1, keepdims=True)
    return zc * lax.rsqrt(var + LN_EPS) * g + b


def _split3(v):
    a0 = v.astype(BF16)
    r1 = v - a0.astype(F32)
    a1 = r1.astype(BF16)
    a2 = (r1 - a1.astype(F32)).astype(BF16)
    return a0, a1, a2


def _inproj_even_kernel(x_ref, w_ref, wf_ref, bf_ref, gates_ref, qkv_ref, csplit_ref, carry_ref,
                        *, tiles_per_seq):
    i = pl.program_id(0)
    x = x_ref[...].astype(BF16)
    n_gate = gates_ref.shape[1]
    for j in range(n_gate // COL_CHUNK):
        sl = slice(j * COL_CHUNK, (j + 1) * COL_CHUNK)
        gates_ref[:, sl] = jnp.dot(x, w_ref[:, sl], preferred_element_type=F32)
    for j in range(qkv_ref.shape[1] // COL_CHUNK):
        sl = slice(j * COL_CHUNK, (j + 1) * COL_CHUNK)
        wsl = slice(n_gate + j * COL_CHUNK, n_gate + (j + 1) * COL_CHUNK)
        qkv_ref[:, sl] = jnp.dot(x, w_ref[:, wsl], preferred_element_type=F32).astype(BF16)

    z = jnp.dot(x, wf_ref[...], preferred_element_type=F32) + bf_ref[...]
    lf = jnp.minimum(z, 0.0) - jnp.log1p(jnp.exp(-jnp.abs(z)))
    tm = lf.shape[0]
    r = lax.broadcasted_iota(jnp.int32, (tm, tm), 0)
    c = lax.broadcasted_iota(jnp.int32, (tm, tm), 1)
    tri = jnp.where(c <= r, 1.0, 0.0).astype(BF16)
    a0, a1, a2 = _split3(lf)
    cs = (jnp.dot(tri, a0, preferred_element_type=F32)
          + jnp.dot(tri, a1, preferred_element_type=F32)
          + jnp.dot(tri, a2, preferred_element_type=F32))

    @pl.when(i % tiles_per_seq == 0)
    def _():
        carry_ref[...] = jnp.zeros_like(carry_ref)

    cs = cs + carry_ref[0:1, :]
    carry_ref[...] = jnp.broadcast_to(cs[tm - 1:tm, :], carry_ref.shape)
    c0, c1, c2 = _split3(cs * LOG2E)
    lane = lax.broadcasted_iota(jnp.int32, (tm, LANES), 1)
    csplit_ref[:, 0:LANES] = c0
    csplit_ref[:, LANES:2 * LANES] = c1
    csplit_ref[:, 2 * LANES:3 * LANES] = c2
    csplit_ref[:, 3 * LANES:4 * LANES] = jnp.where(lane == 0, 1.0, 0.0).astype(BF16)


def _inproj_even(xf, w_main, wf, bf, *, seq):
    n, d = xf.shape
    tm = min(PROJ_TILE, seq)
    n_gate = 3 * (d // 2)
    n_qkv = w_main.shape[1] - n_gate
    return pl.pallas_call(
        functools.partial(_inproj_even_kernel, tiles_per_seq=seq // tm),
        out_shape=(jax.ShapeDtypeStruct((n, n_gate), F32),
                   jax.ShapeDtypeStruct((n, n_qkv), BF16),
                   jax.ShapeDtypeStruct((n, 4 * LANES), BF16)),
        grid=(n // tm,),
        in_specs=[pl.BlockSpec((tm, d), lambda i: (i, 0)),
                  pl.BlockSpec(w_main.shape, lambda i: (0, 0)),
                  pl.BlockSpec(wf.shape, lambda i: (0, 0)),
                  pl.BlockSpec(bf.shape, lambda i: (0, 0))],
        out_specs=(pl.BlockSpec((tm, n_gate), lambda i: (i, 0)),
                   pl.BlockSpec((tm, n_qkv), lambda i: (i, 0)),
                   pl.BlockSpec((tm, 4 * LANES), lambda i: (i, 0))),
        scratch_shapes=[pltpu.VMEM((8, LANES), F32)],
        compiler_params=_params(1),
        name="inproj_even",
    )(xf, w_main, wf, bf)


def _inproj_odd_kernel(x_ref, w_ref, h_ref):
    x = x_ref[...].astype(BF16)
    for j in range(h_ref.shape[1] // COL_CHUNK):
        sl = slice(j * COL_CHUNK, (j + 1) * COL_CHUNK)
        h_ref[:, sl] = jnp.dot(x, w_ref[:, sl], preferred_element_type=F32).astype(BF16)


def _inproj_odd(xf, w):
    n, d = xf.shape
    tm = PROJ_TILE
    return pl.pallas_call(
        _inproj_odd_kernel,
        out_shape=jax.ShapeDtypeStruct((n, w.shape[1]), BF16),
        grid=(n // tm,),
        in_specs=[pl.BlockSpec((tm, d), lambda i: (i, 0)),
                  pl.BlockSpec(w.shape, lambda i: (0, 0))],
        out_specs=pl.BlockSpec((tm, w.shape[1]), lambda i: (i, 0)),
        compiler_params=_params(1),
        name="inproj_odd",
    )(xf, w)


def _flash_scratch(tq, seq):
    r2 = 2 * tq
    return ([pltpu.VMEM((seq, 2 * LANES), BF16)]
            + [pltpu.VMEM((r2, tq), F32) for _ in range(2)]
            + [pltpu.VMEM((r2, tq), BF16) for _ in range(2)]
            + [pltpu.VMEM((r2, LANES), F32) for _ in range(2)]
            + [pltpu.VMEM((r2, LANES), F32) for _ in range(3)])


def _stack_queries(q, feat_a, feat_b):
    lane = lax.broadcasted_iota(jnp.int32, q.shape, 1)
    zero = jnp.zeros_like(q)
    qa = jnp.concatenate([jnp.where(lane < HEAD_DIM, q, zero), feat_a], axis=1)
    qb = jnp.concatenate([jnp.where(lane >= HEAD_DIM, q, zero), feat_b], axis=1)
    return jnp.concatenate([qa, qb], axis=0)


def _flash_pipeline(qaug, kaug_ref, v_ref, scratch, qi, tq, diag_bias_ref):
    s_x, s_y, p_x, p_y, al_x, al_y, m_scr, l_scr, acc_scr = scratch
    r2 = 2 * tq
    n_rep = tq // LANES

    def scores(j, s_ref):
        off = pl.multiple_of(j * tq, tq)
        s_ref[...] = lax.dot_general(qaug, kaug_ref[pl.ds(off, tq), :], _NT,
                                     preferred_element_type=F32)

    def weighted_values(j, p_ref, al_ref):
        off = pl.multiple_of(j * tq, tq)
        acc_scr[...] = al_ref[...] * acc_scr[...] + jnp.dot(
            p_ref[...], v_ref[pl.ds(off, tq), :], preferred_element_type=F32)

    def softmax(s_ref, p_ref, al_ref, bias_fn):
        for c in range(r2 // SOFTMAX_ROWS):
            rows = slice(c * SOFTMAX_ROWS, (c + 1) * SOFTMAX_ROWS)
            x = s_ref[rows, :]
            if bias_fn is not None:
                x = x + bias_fn(c)
            m_old = m_scr[rows, :]
            m_new = jnp.maximum(m_old, jnp.max(x, axis=1, keepdims=True))
            alpha = jnp.exp2(m_old - m_new)
            p = jnp.exp2(x - jnp.concatenate([m_new] * n_rep, axis=1))
            p_sum = p[:, 0:LANES]
            for t in range(1, n_rep):
                p_sum = p_sum + p[:, t * LANES:(t + 1) * LANES]
            m_scr[rows, :] = m_new
            l_scr[rows, :] = alpha * l_scr[rows, :] + p_sum
            al_ref[rows, :] = alpha
            p_ref[rows, :] = p.astype(BF16)

    def diag_bias(c):
        r0 = (c * SOFTMAX_ROWS) % tq
        return diag_bias_ref[r0:r0 + SOFTMAX_ROWS, :]

    scores(0, s_x)
    m_scr[...] = jnp.full(m_scr.shape, M_INIT, F32)
    l_scr[...] = jnp.zeros(l_scr.shape, F32)
    acc_scr[...] = jnp.zeros(acc_scr.shape, F32)
    p_y[...] = jnp.zeros(p_y.shape, BF16)
    al_y[...] = jnp.ones(al_y.shape, F32)

    def pair(i, carry):
        e = 2 * i
        weighted_values(jnp.maximum(e - 1, 0), p_y, al_y)
        scores(e + 1, s_y)
        softmax(s_x, p_x, al_x, None)
        weighted_values(e, p_x, al_x)
        scores(e + 2, s_x)
        softmax(s_y, p_y, al_y, None)
        return carry

    lax.fori_loop(0, qi // 2, pair, 0)

    e = 2 * (qi // 2)
    gate = jnp.where(qi % 2 == 1, 0.0, MASKED).astype(F32)
    weighted_values(jnp.maximum(e - 1, 0), p_y, al_y)
    scores(qi, s_y)
    softmax(s_x, p_x, al_x, lambda c: gate)
    weighted_values(e, p_x, al_x)
    softmax(s_y, p_y, al_y, diag_bias)
    weighted_values(qi, p_y, al_y)
    return jnp.sum(l_scr[...], axis=1, keepdims=True), acc_scr[...]


def _fox_kernel(q_ref, k_ref, v_ref, cs_ref, eq_ref, ek_ref, o_ref, diag_ref, kaug_ref, *scratch,
                tq):
    kaug_ref[:, 0:LANES] = k_ref[...]
    kaug_ref[:, LANES:] = jnp.dot(cs_ref[...], ek_ref[0], preferred_element_type=F32).astype(BF16)
    row = lax.broadcasted_iota(jnp.int32, (tq, tq), 0)
    col = lax.broadcasted_iota(jnp.int32, (tq, tq), 1)
    diag_ref[...] = jnp.where(col <= row, 0.0, MASKED)

    def query_tile(qi, carry):
        rows = pl.ds(pl.multiple_of(qi * tq, tq), tq)
        cs_q = cs_ref[rows, :]
        feats = [jnp.dot(cs_q, eq_ref[0, sub], preferred_element_type=F32).astype(BF16)
                 for sub in range(2)]
        qaug = _stack_queries(q_ref[rows, :], feats[0], feats[1])
        l, acc = _flash_pipeline(qaug, kaug_ref, v_ref, scratch, qi, tq, diag_ref)
        out = acc / l
        lane = lax.broadcasted_iota(jnp.int32, (tq, LANES), 1)
        o_ref[rows, :] = jnp.where(lane < HEAD_DIM, out[0:tq], out[tq:2 * tq]).astype(o_ref.dtype)
        return carry

    lax.fori_loop(0, q_ref.shape[0] // tq, query_tile, 0)


def _fox_feature_maps(n_pairs):
    one_row = 3 * LANES
    ek = np.zeros((n_pairs, 4 * LANES, LANES), np.float32)
    eq = np.zeros((n_pairs, 2, 4 * LANES, LANES), np.float32)
    for p in range(n_pairs):
        for sub in range(2):
            head, base = 2 * p + sub, 6 * sub
            for t in range(3):
                ek[p, one_row, base + t] = 1.0
                ek[p, t * LANES + head, base + 3 + t] = -1.0
                eq[p, sub, t * LANES + head, base + t] = 1.0
                eq[p, sub, one_row, base + 3 + t] = 1.0
    return jnp.asarray(eq, BF16), jnp.asarray(ek, BF16)


def _fox_attention(qkv, csplit, *, batch, seq):
    n, width3 = qkv.shape
    n_pairs = width3 // 3 // LANES
    tq = min(ATTN_TILE, seq)
    eq, ek = _fox_feature_maps(n_pairs)
    return pl.pallas_call(
        functools.partial(_fox_kernel, tq=tq),
        out_shape=jax.ShapeDtypeStruct((n, n_pairs * LANES), BF16),
        grid=(batch, n_pairs),
        in_specs=[pl.BlockSpec((seq, LANES), lambda b, h: (b, h)),
                  pl.BlockSpec((seq, LANES), lambda b, h: (b, n_pairs + h)),
                  pl.BlockSpec((seq, LANES), lambda b, h: (b, 2 * n_pairs + h)),
                  pl.BlockSpec((seq, csplit.shape[1]), lambda b, h: (b, 0)),
                  pl.BlockSpec((1,) + eq.shape[1:], lambda b, h: (h, 0, 0, 0)),
                  pl.BlockSpec((1,) + ek.shape[1:], lambda b, h: (h, 0, 0))],
        out_specs=pl.BlockSpec((seq, LANES), lambda b, h: (b, h)),
        scratch_shapes=[pltpu.VMEM((tq, tq), F32)] + _flash_scratch(tq, seq),
        compiler_params=_params(2),
        name="fox_attention",
    )(qkv, qkv, qkv, csplit, eq, ek)


def _diff_kernel(slopes_ref, q_ref, k_ref, v_ref, lam_ref, g_ref, o_ref, diag_ref, kaug_ref,
                 *scratch, tq, lam_init):
    h = pl.program_id(1)
    seq = k_ref.shape[0]
    slope2 = slopes_ref[h] * LOG2E

    def bias_lanes(n_rows, first, base, other, other_value):
        lane = lax.broadcasted_iota(jnp.int32, (n_rows, LANES), 1)
        pos = (lax.broadcasted_iota(jnp.int32, (n_rows, LANES), 0) + first).astype(F32) * slope2
        terms = [t.astype(F32) for t in _split3(pos)]
        feat = jnp.where((lane >= other) & (lane < other + 3), other_value, 0.0)
        for t, term in enumerate(terms):
            feat = jnp.where(lane == base + t, term, feat)
        return feat.astype(BF16)

    kaug_ref[:, 0:LANES] = k_ref[...]
    kaug_ref[:, LANES:] = bias_lanes(seq, 0, 0, 3, -1.0)
    row = lax.broadcasted_iota(jnp.int32, (tq, tq), 0)
    col = lax.broadcasted_iota(jnp.int32, (tq, tq), 1)
    ahead = jnp.maximum(col - row, 0).astype(F32)
    diag_ref[...] = jnp.where((col // CHUNK) <= (row // CHUNK), -2.0 * slope2 * ahead, MASKED)
    t1 = jnp.sum(lam_ref[0:1, :] * lam_ref[1:2, :], axis=-1, keepdims=True)
    t2 = jnp.sum(lam_ref[2:3, :] * lam_ref[3:4, :], axis=-1, keepdims=True)
    lam = jnp.exp(t1) - jnp.exp(t2) + lam_init

    def query_tile(qi, carry):
        rows = pl.ds(pl.multiple_of(qi * tq, tq), tq)
        qfeat = bias_lanes(tq, qi * tq, 3, 0, 1.0)
        qaug = _stack_queries(q_ref[rows, :], qfeat, qfeat)
        l, acc = _flash_pipeline(qaug, kaug_ref, v_ref, scratch, qi, tq, diag_ref)
        out = acc / l
        o = out[0:tq] - lam * out[tq:2 * tq]
        o = o * lax.rsqrt(jnp.mean(o * o, axis=-1, keepdims=True) + RMS_EPS)
        o_ref[rows, :] = (o * g_ref[...] * (1.0 - lam_init)).astype(o_ref.dtype)
        return carry

    lax.fori_loop(0, seq // tq, query_tile, 0)


def _diff_attention(h, slopes, lam_rows, subln_g, *, batch, seq, lam_init):
    n, width3 = h.shape
    n_heads = width3 // 3 // LANES
    tq = min(ATTN_TILE, seq)
    grid_spec = pltpu.PrefetchScalarGridSpec(
        num_scalar_prefetch=1,
        grid=(batch, n_heads),
        in_specs=[pl.BlockSpec((seq, LANES), lambda b, hd, s: (b, hd)),
                  pl.BlockSpec((seq, LANES), lambda b, hd, s: (b, n_heads + hd)),
                  pl.BlockSpec((seq, LANES), lambda b, hd, s: (b, 2 * n_heads + hd)),
                  pl.BlockSpec(lam_rows.shape, lambda b, hd, s: (0, 0)),
                  pl.BlockSpec(subln_g.shape, lambda b, hd, s: (0, 0))],
        out_specs=pl.BlockSpec((seq, LANES), lambda b, hd, s: (b, hd)),
        scratch_shapes=[pltpu.VMEM((tq, tq), F32)] + _flash_scratch(tq, seq),
    )
    return pl.pallas_call(
        functools.partial(_diff_kernel, tq=tq, lam_init=lam_init),
        out_shape=jax.ShapeDtypeStruct((n, n_heads * LANES), BF16),
        grid_spec=grid_spec,
        compiler_params=_params(2),
        name="diff_attention",
    )(slopes, h, h, h, lam_rows, subln_g)


def _route(lg):
    lane = lax.broadcasted_iota(jnp.int32, lg.shape, 1).astype(F32)
    low = jnp.float32(-3e38)
    is_g = lane < N_GROUPS
    gmax = jnp.max(jnp.where(is_g, lg, low), axis=-1, keepdims=True)
    gsel = jnp.min(jnp.where(is_g & (lg == gmax), lane, float(LANES)), axis=-1, keepdims=True)
    gsum = jnp.sum(jnp.where(is_g, jnp.exp(lg - gmax), 0.0), axis=-1, keepdims=True)
    gw = 1.0 / gsum
    lo = N_GROUPS + EXPERTS_PER_GROUP * gsel
    in_grp = (lane >= lo) & (lane < lo + EXPERTS_PER_GROUP)
    el = jnp.where(in_grp, lg, low)
    v1 = jnp.max(el, axis=-1, keepdims=True)
    i1 = jnp.min(jnp.where(in_grp & (lg == v1), lane, float(LANES)), axis=-1, keepdims=True)
    rest = in_grp & (lane != i1)
    v2 = jnp.max(jnp.where(rest, lg, low), axis=-1, keepdims=True)
    i2 = jnp.min(jnp.where(rest & (lg == v2), lane, float(LANES)), axis=-1, keepdims=True)
    t = jnp.exp(v2 - v1)
    w1 = gw / (1.0 + t)
    w2 = gw * (t / (1.0 + t))
    rec = jnp.where(lane == 0.0, i1 - N_GROUPS,
                    jnp.where(lane == 1.0, i2 - N_GROUPS,
                              jnp.where(lane == 2.0, w1, jnp.where(lane == 3.0, w2, 0.0))))
    return rec


def _mix_ln_route(mix, x_ref, lng_ref, lnb_ref, wr_ref, br_ref, xo_ref, rec_ref, alpha):
    xo = _layer_norm(alpha * x_ref[...] + mix, lng_ref[...], lnb_ref[...])
    xo_ref[...] = xo
    d = xo.shape[1]
    x_hi = xo.astype(BF16)
    x_lo = (xo - x_hi.astype(F32)).astype(BF16)
    lg = (jnp.dot(x_hi, wr_ref[0:d, :], preferred_element_type=F32)
          + jnp.dot(x_lo, wr_ref[0:d, :], preferred_element_type=F32)
          + jnp.dot(x_hi, wr_ref[d:2 * d, :], preferred_element_type=F32)) + br_ref[...]
    rec_ref[...] = _route(lg)


def _outproj_even_kernel(gates_ref, attn_ref, x_ref, wout_ref, convw_ref, lng_ref, lnb_ref,
                         wr_ref, br_ref, xo_ref, rec_ref, tail_ref, *, tiles_per_seq, alpha):
    i = pl.program_id(0)
    tm = x_ref.shape[0]
    cw = gates_ref.shape[1] // 3
    gb = gates_ref[:, 0:cw]
    u = gates_ref[:, cw:2 * cw] * gates_ref[:, 2 * cw:3 * cw]

    @pl.when(i % tiles_per_seq == 0)
    def _():
        tail_ref[...] = jnp.zeros_like(tail_ref)

    tail = tail_ref[...]
    row = lax.broadcasted_iota(jnp.int32, (tm, 1), 0)
    u1 = jnp.where(row == 0, tail[7:8, :], pltpu.roll(u, 1, axis=0))
    u2 = jnp.where(row == 0, tail[6:7, :],
                   jnp.where(row == 1, tail[7:8, :], pltpu.roll(u, 2, axis=0)))
    tail_ref[...] = u[tm - 8:tm, :]
    y = u2 * convw_ref[0:1, :] + u1 * convw_ref[1:2, :] + u * convw_ref[2:3, :]
    a = (gb * y).astype(BF16)
    mix = (jnp.dot(a, wout_ref[0:cw, :], preferred_element_type=F32)
           + jnp.dot(attn_ref[...], wout_ref[cw:, :], preferred_element_type=F32))
    _mix_ln_route(mix, x_ref, lng_ref, lnb_ref, wr_ref, br_ref, xo_ref, rec_ref, alpha)


def _outproj_odd_kernel(attn_ref, x_ref, wout_ref, lng_ref, lnb_ref, wr_ref, br_ref,
                        xo_ref, rec_ref, *, alpha):
    mix = jnp.dot(attn_ref[...], wout_ref[...], preferred_element_type=F32)
    _mix_ln_route(mix, x_ref, lng_ref, lnb_ref, wr_ref, br_ref, xo_ref, rec_ref, alpha)


def _outproj(acts, xf, wout, convw, lng, lnb, wr, br, *, seq, alpha):
    n, d = xf.shape
    tm = min(ROW_TILE, seq)
    row_spec = lambda width: pl.BlockSpec((tm, width), lambda i: (i, 0))
    full = lambda a: pl.BlockSpec(a.shape, lambda i: (0, 0))
    even = convw is not None
    if even:
        body = functools.partial(_outproj_even_kernel, tiles_per_seq=seq // tm, alpha=alpha)
        operands = (*acts, xf, wout, convw, lng, lnb, wr, br)
        scratch = [pltpu.VMEM((8, acts[0].shape[1] // 3), F32)]
    else:
        body = functools.partial(_outproj_odd_kernel, alpha=alpha)
        operands = (*acts, xf, wout, lng, lnb, wr, br)
        scratch = []
    in_specs = [row_spec(a.shape[1]) for a in acts] + [row_spec(d)]
    in_specs += [full(a) for a in operands[len(acts) + 1:]]
    return pl.pallas_call(
        body,
        out_shape=(jax.ShapeDtypeStruct((n, d), F32), jax.ShapeDtypeStruct((n, LANES), F32)),
        grid=(n // tm,),
        in_specs=in_specs,
        out_specs=(row_spec(d), row_spec(LANES)),
        scratch_shapes=scratch,
        compiler_params=_params(1),
        name="outproj_even" if even else "outproj_odd",
    )(*operands)


def _plan_kernel(rec_ref, start_ref, pos_ref, run_ref):
    i = pl.program_id(0)

    @pl.when(i == 0)
    def _():
        run_ref[...] = jnp.zeros_like(run_ref)

    rec = rec_ref[...]
    tm = rec.shape[0]
    lane = lax.broadcasted_iota(jnp.int32, rec.shape, 1).astype(F32)
    oh1 = lane == rec[:, 0:1]
    oh2 = lane == rec[:, 1:2]
    oh = jnp.where(oh1 | oh2, 1.0, 0.0)
    r = lax.broadcasted_iota(jnp.int32, (tm, tm), 0)
    c = lax.broadcasted_iota(jnp.int32, (tm, tm), 1)
    earlier = jnp.where(c < r, 1.0, 0.0).astype(BF16)
    base = (jnp.dot(earlier, oh.astype(BF16), preferred_element_type=F32)
            + run_ref[0:1, :] + start_ref[...])
    p1 = jnp.sum(jnp.where(oh1, base, 0.0), axis=1, keepdims=True)
    p2 = jnp.sum(jnp.where(oh2, base, 0.0), axis=1, keepdims=True)
    run_ref[...] = run_ref[...] + jnp.sum(oh, axis=0, keepdims=True)
    pos_ref[...] = jnp.where(lane == 0.0, p1, jnp.where(lane == 1.0, p2, 0.0)).astype(jnp.int32)


def _plan_rows(rec, seg_start):
    n = rec.shape[0]
    tm = ROW_TILE
    return pl.pallas_call(
        _plan_kernel,
        out_shape=jax.ShapeDtypeStruct((n, LANES), jnp.int32),
        grid=(n // tm,),
        in_specs=[pl.BlockSpec((tm, LANES), lambda i: (i, 0)),
                  pl.BlockSpec(seg_start.shape, lambda i: (0, 0))],
        out_specs=pl.BlockSpec((tm, LANES), lambda i: (i, 0)),
        scratch_shapes=[pltpu.VMEM((8, LANES), F32)],
        compiler_params=_params(1),
        name="moe_plan",
    )(rec, seg_start)


def _dispatch_plan(rec, n_tok, rows):
    eid = rec[:, 0:2].astype(jnp.int32).reshape(-1)
    counts = jnp.sum((eid[:, None] == jnp.arange(N_EXPERTS, dtype=jnp.int32)[None, :])
                     .astype(jnp.int32), axis=0)
    padded = (counts + rows - 1) // rows * rows
    pad_end = jnp.cumsum(padded)
    seg_start = jnp.pad((pad_end - padded).astype(F32), (0, LANES - N_EXPERTS))[None, :]
    pos = _plan_rows(rec, seg_start)[:, 0:2].reshape(-1)
    cap = 2 * n_tok + N_EXPERTS * rows
    n_blocks = cap // rows
    n_pairs = 2 * n_tok
    row_pair = jnp.full((cap,), -1, jnp.int32).at[pos].set(jnp.arange(n_pairs, dtype=jnp.int32))
    src = jnp.where(row_pair >= 0, row_pair >> 1, 0)
    look_ahead = GATHER_BUFFERS - 1
    src_ext = jnp.concatenate([src, jnp.zeros((look_ahead * rows,), jnp.int32)])
    src_ext = src_ext.reshape(n_blocks + look_ahead, 1, rows)
    tm = ROW_TILE
    n_tiles = n_tok // tm
    pos_ext = jnp.transpose(pos.reshape(n_tiles, tm, 2), (0, 2, 1)).reshape(n_tiles, 2 * tm)
    pos_ext = jnp.concatenate([pos_ext, jnp.zeros((1, 2 * tm), jnp.int32)])
    pos_ext = pos_ext.reshape(n_tiles + 1, 1, 2 * tm)
    blk_start = jnp.arange(n_blocks, dtype=jnp.int32) * rows
    blk_e = jnp.minimum(jnp.sum((pad_end[None, :] <= blk_start[:, None]).astype(jnp.int32), axis=1),
                        N_EXPERTS - 1)
    return src_ext, pos_ext, blk_e


def _start_rows(src_hbm, idx_ref, idx_base, dst_ref, slot, sem, n):
    for r in range(n):
        pltpu.make_async_copy(src_hbm.at[pl.ds(idx_ref[0, 0, idx_base + r], 1)],
                              dst_ref.at[slot, pl.ds(r, 1)], sem.at[slot]).start(priority=r % 2)


def _wait_rows(src_hbm, dst_ref, slot, sem):
    n = dst_ref.shape[1]
    pltpu.make_async_copy(src_hbm.at[pl.ds(0, n)], dst_ref.at[slot], sem.at[slot]).wait()


def _expert_mlp_kernel(blk_e_ref, src0_ref, src1_ref, src2_ref, x_hbm, wg_ref, wu_ref, wd_ref,
                       out_ref, xbuf, wgu_s, wd_s, sem):
    i = pl.program_id(0)
    rows = xbuf.shape[1]
    dh = wg_ref.shape[3]
    slot = i % GATHER_BUFFERS
    ahead1 = (i + 1) % GATHER_BUFFERS
    ahead2 = (i + 2) % GATHER_BUFFERS

    @pl.when(i == 0)
    def _():
        _start_rows(x_hbm, src0_ref, 0, xbuf, 0, sem, rows)
        _start_rows(x_hbm, src1_ref, 0, xbuf, 1, sem, rows)

    _wait_rows(x_hbm, xbuf, slot, sem)
    changed = jnp.logical_or(i == 0, blk_e_ref[i] != blk_e_ref[jnp.maximum(i - 1, 0)])

    @pl.when(changed)
    def _():
        wgu_s[:, 0:dh] = wg_ref[0, 0].astype(BF16)
        wgu_s[:, dh:2 * dh] = wu_ref[0, 0].astype(BF16)
        wd_s[...] = wd_ref[0, 0].astype(BF16)

    x = xbuf[slot].astype(BF16)
    _start_rows(x_hbm, src2_ref, 0, xbuf, ahead2, sem, rows)
    gu = jnp.dot(x, wgu_s[...], preferred_element_type=F32)
    g = gu[:, 0:dh]
    hid = (g / (1.0 + jnp.exp(-g))) * gu[:, dh:2 * dh]
    out_ref[...] = jnp.dot(hid.astype(BF16), wd_s[...], preferred_element_type=F32)

    @pl.when(i == pl.num_programs(0) - 1)
    def _():
        _wait_rows(x_hbm, xbuf, ahead1, sem)
        _wait_rows(x_hbm, xbuf, ahead2, sem)


def _expert_mlp(xf, w_gate, w_up, w_down, layer, src_ext, blk_e, rows):
    n_tok, d = xf.shape
    dh = w_gate.shape[3]
    n_blocks = blk_e.shape[0]
    idx_spec = lambda shift: pl.BlockSpec((1, 1, rows), lambda i, be: (i + shift, 0, 0),
                                          memory_space=pltpu.SMEM)
    grid_spec = pltpu.PrefetchScalarGridSpec(
        num_scalar_prefetch=1,
        grid=(n_blocks,),
        in_specs=[idx_spec(0), idx_spec(1), idx_spec(2),
                  pl.BlockSpec(memory_space=pl.ANY),
                  pl.BlockSpec((1, 1, d, dh), lambda i, be: (layer, be[i], 0, 0)),
                  pl.BlockSpec((1, 1, d, dh), lambda i, be: (layer, be[i], 0, 0)),
                  pl.BlockSpec((1, 1, dh, d), lambda i, be: (layer, be[i], 0, 0))],
        out_specs=pl.BlockSpec((rows, d), lambda i, be: (i, 0)),
        scratch_shapes=[pltpu.VMEM((GATHER_BUFFERS, rows, d), F32),
                        pltpu.VMEM((d, 2 * dh), BF16), pltpu.VMEM((dh, d), BF16),
                        pltpu.SemaphoreType.DMA((GATHER_BUFFERS,))],
    )
    return pl.pallas_call(
        _expert_mlp_kernel,
        out_shape=jax.ShapeDtypeStruct((n_blocks * rows, d), F32),
        grid_spec=grid_spec,
        compiler_params=_params(1),
        name="moe_experts",
    )(blk_e, src_ext, src_ext, src_ext, xf, w_gate, w_up, w_down)


def _combine_ln_kernel(pos0_ref, pos1_ref, yb_hbm, rec_ref, x_ref, lng_ref, lnb_ref, xo_ref,
                       ya_buf, yb_buf, sem_a, sem_b, *, alpha):
    i = pl.program_id(0)
    tm = x_ref.shape[0]
    slot = i % 2

    @pl.when(i == 0)
    def _():
        _start_rows(yb_hbm, pos0_ref, 0, ya_buf, 0, sem_a, tm)
        _start_rows(yb_hbm, pos0_ref, tm, yb_buf, 0, sem_b, tm)

    _wait_rows(yb_hbm, ya_buf, slot, sem_a)
    _wait_rows(yb_hbm, yb_buf, slot, sem_b)
    z = alpha * x_ref[...] + (rec_ref[:, 2:3] * ya_buf[slot] + rec_ref[:, 3:4] * yb_buf[slot])
    _start_rows(yb_hbm, pos1_ref, 0, ya_buf, 1 - slot, sem_a, tm)
    _start_rows(yb_hbm, pos1_ref, tm, yb_buf, 1 - slot, sem_b, tm)
    xo_ref[...] = _layer_norm(z, lng_ref[...], lnb_ref[...])

    @pl.when(i == pl.num_programs(0) - 1)
    def _():
        _wait_rows(yb_hbm, ya_buf, 1 - slot, sem_a)
        _wait_rows(yb_hbm, yb_buf, 1 - slot, sem_b)


def _combine_ln(yb, pos_ext, rec, xf, lng, lnb, *, alpha):
    n, d = xf.shape
    tm = ROW_TILE
    idx_spec = lambda shift: pl.BlockSpec((1, 1, 2 * tm), lambda i: (i + shift, 0, 0),
                                          memory_space=pltpu.SMEM)
    return pl.pallas_call(
        functools.partial(_combine_ln_kernel, alpha=alpha),
        out_shape=jax.ShapeDtypeStruct((n, d), F32),
        grid=(n // tm,),
        in_specs=[idx_spec(0), idx_spec(1),
                  pl.BlockSpec(memory_space=pl.ANY),
                  pl.BlockSpec((tm, LANES), lambda i: (i, 0)),
                  pl.BlockSpec((tm, d), lambda i: (i, 0)),
                  pl.BlockSpec(lng.shape, lambda i: (0, 0)),
                  pl.BlockSpec(lnb.shape, lambda i: (0, 0))],
        out_specs=pl.BlockSpec((tm, d), lambda i: (i, 0)),
        scratch_shapes=[pltpu.VMEM((2, tm, d), F32), pltpu.VMEM((2, tm, d), F32),
                        pltpu.SemaphoreType.DMA((2,)), pltpu.SemaphoreType.DMA((2,))],
        compiler_params=_params(1),
        name="moe_combine_ln",
    )(pos_ext, pos_ext, yb, rec, xf, lng, lnb)


def kernel(x, ab_w_in, ab_b_forget, ab_conv_w, ab_w_out, c_w_in, c_lam_q1, c_lam_k1, c_lam_q2,
           c_lam_k2, c_subln_g, c_w_out, ln_mix_g, ln_mix_b, ln_ffn_g, ln_ffn_b, moe_w_group,
           moe_b_group, moe_w_expert, moe_b_expert, moe_w_gate, moe_w_up, moe_w_down):
    batch, seq, d = x.shape
    depth = ln_mix_g.shape[0]
    n_tok = batch * seq
    alpha = (2.0 * depth) ** 0.25
    q_scale = HEAD_DIM ** -0.5 * LOG2E
    conv_ch = ab_conv_w.shape[1]
    fox_heads = ab_b_forget.shape[1]
    n_diff_heads = d // (2 * HEAD_DIM)
    slopes = jnp.asarray(2.0 ** (-8.0 * np.arange(1, n_diff_heads + 1) / n_diff_heads), F32)
    xf = x.reshape(n_tok, d)

    def pad_lanes(v):
        return jnp.pad(v, (0, LANES - v.shape[0]))

    for layer in range(depth):
        i = layer // 2
        w_router = jnp.pad(jnp.concatenate([moe_w_group[layer], moe_w_expert[layer]], axis=1),
                           ((0, 0), (0, LANES - N_GROUPS - N_EXPERTS)))
        w_router_hi = w_router.astype(BF16)
        w_router = jnp.concatenate(
            [w_router_hi, (w_router - w_router_hi.astype(F32)).astype(BF16)], axis=0)
        b_router = pad_lanes(jnp.concatenate([moe_b_group[layer], moe_b_expert[layer]]))[None, :]
        lng, lnb = ln_mix_g[layer][None, :], ln_mix_b[layer][None, :]
        if layer % 2 == 0:
            n_main = ab_w_in.shape[2] - fox_heads
            q_lo = 3 * conv_ch
            col_scale = np.ones((n_main,), np.float32)
            col_scale[q_lo:q_lo + fox_heads * HEAD_DIM] = q_scale
            w_main = (ab_w_in[i, :, :n_main] * col_scale).astype(BF16)
            wf = jnp.pad(ab_w_in[i, :, n_main:], ((0, 0), (0, LANES - fox_heads))).astype(BF16)
            gates, qkv, csplit = _inproj_even(xf, w_main, wf, pad_lanes(ab_b_forget[i])[None, :],
                                              seq=seq)
            attn = _fox_attention(qkv, csplit, batch=batch, seq=seq)
            convw = jnp.pad(jnp.transpose(ab_conv_w[i]), ((0, 8 - ab_conv_w.shape[2]), (0, 0)))
            xf, rec = _outproj((gates, attn), xf, ab_w_out[i].astype(BF16), convw, lng, lnb,
                               w_router, b_router, seq=seq, alpha=alpha)
        else:
            col_scale = np.ones((c_w_in.shape[2],), np.float32)
            col_scale[0:d] = q_scale
            h = _inproj_odd(xf, (c_w_in[i] * col_scale).astype(BF16))
            lam_init = 0.8 - 0.6 * math.exp(-0.3 * layer)
            lam_rows = jnp.pad(jnp.stack([pad_lanes(c_lam_q1[i]), pad_lanes(c_lam_k1[i]),
                                          pad_lanes(c_lam_q2[i]), pad_lanes(c_lam_k2[i])]),
                               ((0, 4), (0, 0)))
            attn = _diff_attention(h, slopes, lam_rows, c_subln_g[i][None, :],
                                   batch=batch, seq=seq, lam_init=lam_init)
            xf, rec = _outproj((attn,), xf, c_w_out[i].astype(BF16), None, lng, lnb,
                               w_router, b_router, seq=seq, alpha=alpha)

        src_ext, pos_ext, blk_e = _dispatch_plan(rec, n_tok, MOE_ROWS)
        yb = _expert_mlp(xf, moe_w_gate, moe_w_up, moe_w_down, layer, src_ext, blk_e, MOE_ROWS)
        xf = _combine_ln(yb, pos_ext, rec, xf, ln_ffn_g[layer][None, :],
                         ln_ffn_b[layer][None, :], alpha=alpha)
    return xf.reshape(batch, seq, d)
```

```python
import functools
import math

import numpy as np
import jax
import jax.numpy as jnp
from jax import lax
from jax.experimental import pallas as pl
from jax.experimental.pallas import tpu as pltpu

F32 = jnp.float32
BF16 = jnp.bfloat16

LANES = 128
HEAD_DIM = 64
N_GROUPS = 4
EXPERTS_PER_GROUP = 8
N_EXPERTS = N_GROUPS * EXPERTS_PER_GROUP
CHUNK = 64
LN_EPS = 1e-5
RMS_EPS = 1e-5
LOG2E = math.log2(math.e)
M_INIT = -1e30
MASKED = -3e38
VMEM_LIMIT = 48 * 1024 * 1024

ROW_TILE = 256
PROJ_TILE = 512
ATTN_TILE = 256
SOFTMAX_ROWS = 32
MOE_ROWS = 256
GATHER_BUFFERS = 3
ROW_SEMS = 4
COL_CHUNK = 512

_NT = (((1,), (1,)), ((), ()))


def _params(n_grid):
    return pltpu.CompilerParams(dimension_semantics=("arbitrary",) * n_grid,
                                vmem_limit_bytes=VMEM_LIMIT)


def _layer_norm(z, g, b):
    mu = jnp.mean(z, axis=-1, keepdims=True)
    zc = z - mu
    var = jnp.mean(zc * zc, axis=-1, keepdims=True)
    return zc * lax.rsqrt(var + LN_EPS) * g + b


def _split3(v):
    a0 = v.astype(BF16)
    r1 = v - a0.astype(F32)
    a1 = r1.astype(BF16)
    a2 = (r1 - a1.astype(F32)).astype(BF16)
    return a0, a1, a2


def _project_even(x, w_ref, wf_ref, bf_ref, gates_ref, qkv_ref, csplit_ref, carry_ref,
                  tiles_per_seq):
    i = pl.program_id(0)
    n_gate = gates_ref.shape[1]
    for j in range(n_gate // COL_CHUNK):
        sl = slice(j * COL_CHUNK, (j + 1) * COL_CHUNK)
        gates_ref[:, sl] = jnp.dot(x, w_ref[:, sl], preferred_element_type=F32)
    for j in range(qkv_ref.shape[1] // COL_CHUNK):
        sl = slice(j * COL_CHUNK, (j + 1) * COL_CHUNK)
        wsl = slice(n_gate + j * COL_CHUNK, n_gate + (j + 1) * COL_CHUNK)
        qkv_ref[:, sl] = jnp.dot(x, w_ref[:, wsl], preferred_element_type=F32).astype(BF16)

    z = jnp.dot(x, wf_ref[...], preferred_element_type=F32) + bf_ref[...]
    lf = jnp.minimum(z, 0.0) - jnp.log1p(jnp.exp(-jnp.abs(z)))
    tm = lf.shape[0]
    r = lax.broadcasted_iota(jnp.int32, (tm, tm), 0)
    c = lax.broadcasted_iota(jnp.int32, (tm, tm), 1)
    tri = jnp.where(c <= r, 1.0, 0.0).astype(BF16)
    a0, a1, a2 = _split3(lf)
    cs = (jnp.dot(tri, a0, preferred_element_type=F32)
          + jnp.dot(tri, a1, preferred_element_type=F32)
          + jnp.dot(tri, a2, preferred_element_type=F32))

    @pl.when(i % tiles_per_seq == 0)
    def _():
        carry_ref[...] = jnp.zeros_like(carry_ref)

    cs = cs + carry_ref[0:1, :]
    carry_ref[...] = jnp.broadcast_to(cs[tm - 1:tm, :], carry_ref.shape)
    c0, c1, c2 = _split3(cs * LOG2E)
    lane = lax.broadcasted_iota(jnp.int32, (tm, LANES), 1)
    csplit_ref[:, 0:LANES] = c0
    csplit_ref[:, LANES:2 * LANES] = c1
    csplit_ref[:, 2 * LANES:3 * LANES] = c2
    csplit_ref[:, 3 * LANES:4 * LANES] = jnp.where(lane == 0, 1.0, 0.0).astype(BF16)


def _project_odd(x, w_ref, h_ref):
    for j in range(h_ref.shape[1] // COL_CHUNK):
        sl = slice(j * COL_CHUNK, (j + 1) * COL_CHUNK)
        h_ref[:, sl] = jnp.dot(x, w_ref[:, sl], preferred_element_type=F32).astype(BF16)


def _even_projection_shapes(n, d, w_main):
    n_gate = 3 * (d // 2)
    return [(n, n_gate, F32), (n, w_main.shape[1] - n_gate, BF16), (n, 4 * LANES, BF16)]


def _inproj_even_kernel(x_ref, w_ref, wf_ref, bf_ref, gates_ref, qkv_ref, csplit_ref, carry_ref,
                        *, tiles_per_seq):
    _project_even(x_ref[...].astype(BF16), w_ref, wf_ref, bf_ref, gates_ref, qkv_ref, csplit_ref,
                  carry_ref, tiles_per_seq)


def _inproj_even(xf, w_main, wf, bf, *, seq):
    n, d = xf.shape
    tm = min(PROJ_TILE, seq)
    outs = _even_projection_shapes(n, d, w_main)
    return pl.pallas_call(
        functools.partial(_inproj_even_kernel, tiles_per_seq=seq // tm),
        out_shape=tuple(jax.ShapeDtypeStruct((r, c), t) for r, c, t in outs),
        grid=(n // tm,),
        in_specs=[pl.BlockSpec((tm, d), lambda i: (i, 0)),
                  pl.BlockSpec(w_main.shape, lambda i: (0, 0)),
                  pl.BlockSpec(wf.shape, lambda i: (0, 0)),
                  pl.BlockSpec(bf.shape, lambda i: (0, 0))],
        out_specs=tuple(pl.BlockSpec((tm, c), lambda i: (i, 0)) for _, c, _ in outs),
        scratch_shapes=[pltpu.VMEM((8, LANES), F32)],
        compiler_params=_params(1),
        name="inproj_even",
    )(xf, w_main, wf, bf)


def _flash_scratch(tq, seq):
    r2 = 2 * tq
    return ([pltpu.VMEM((seq, 2 * LANES), BF16)]
            + [pltpu.VMEM((r2, tq), F32) for _ in range(2)]
            + [pltpu.VMEM((r2, tq), BF16) for _ in range(2)]
            + [pltpu.VMEM((r2, LANES), F32) for _ in range(2)]
            + [pltpu.VMEM((r2, LANES), F32) for _ in range(3)])


def _stack_queries(q, feat_a, feat_b):
    lane = lax.broadcasted_iota(jnp.int32, q.shape, 1)
    zero = jnp.zeros_like(q)
    qa = jnp.concatenate([jnp.where(lane < HEAD_DIM, q, zero), feat_a], axis=1)
    qb = jnp.concatenate([jnp.where(lane >= HEAD_DIM, q, zero), feat_b], axis=1)
    return jnp.concatenate([qa, qb], axis=0)


def _flash_pipeline(qaug, kaug_ref, v_ref, scratch, qi, tq, diag_bias_ref):
    s_x, s_y, p_x, p_y, al_x, al_y, m_scr, l_scr, acc_scr = scratch
    r2 = 2 * tq
    n_rep = tq // LANES

    def scores(j, s_ref):
        off = pl.multiple_of(j * tq, tq)
        s_ref[...] = lax.dot_general(qaug, kaug_ref[pl.ds(off, tq), :], _NT,
                                     preferred_element_type=F32)

    def weighted_values(j, p_ref, al_ref):
        off = pl.multiple_of(j * tq, tq)
        acc_scr[...] = al_ref[...] * acc_scr[...] + jnp.dot(
            p_ref[...], v_ref[pl.ds(off, tq), :], preferred_element_type=F32)

    def softmax(s_ref, p_ref, al_ref, bias_fn):
        for c in range(r2 // SOFTMAX_ROWS):
            rows = slice(c * SOFTMAX_ROWS, (c + 1) * SOFTMAX_ROWS)
            x = s_ref[rows, :]
            if bias_fn is not None:
                x = x + bias_fn(c)
            m_old = m_scr[rows, :]
            m_new = jnp.maximum(m_old, jnp.max(x, axis=1, keepdims=True))
            alpha = jnp.exp2(m_old - m_new)
            p = jnp.exp2(x - jnp.concatenate([m_new] * n_rep, axis=1))
            p_sum = p[:, 0:LANES]
            for t in range(1, n_rep):
                p_sum = p_sum + p[:, t * LANES:(t + 1) * LANES]
            m_scr[rows, :] = m_new
            l_scr[rows, :] = alpha * l_scr[rows, :] + p_sum
            al_ref[rows, :] = alpha
            p_ref[rows, :] = p.astype(BF16)

    def diag_bias(c):
        r0 = (c * SOFTMAX_ROWS) % tq
        return diag_bias_ref[r0:r0 + SOFTMAX_ROWS, :]

    scores(0, s_x)
    m_scr[...] = jnp.full(m_scr.shape, M_INIT, F32)
    l_scr[...] = jnp.zeros(l_scr.shape, F32)
    acc_scr[...] = jnp.zeros(acc_scr.shape, F32)
    p_y[...] = jnp.zeros(p_y.shape, BF16)
    al_y[...] = jnp.ones(al_y.shape, F32)

    def pair(i, carry):
        e = 2 * i
        weighted_values(jnp.maximum(e - 1, 0), p_y, al_y)
        scores(e + 1, s_y)
        softmax(s_x, p_x, al_x, None)
        weighted_values(e, p_x, al_x)
        scores(e + 2, s_x)
        softmax(s_y, p_y, al_y, None)
        return carry

    lax.fori_loop(0, qi // 2, pair, 0)

    e = 2 * (qi // 2)
    gate = jnp.where(qi % 2 == 1, 0.0, MASKED).astype(F32)
    weighted_values(jnp.maximum(e - 1, 0), p_y, al_y)
    scores(qi, s_y)
    softmax(s_x, p_x, al_x, lambda c: gate)
    weighted_values(e, p_x, al_x)
    softmax(s_y, p_y, al_y, diag_bias)
    weighted_values(qi, p_y, al_y)
    return jnp.sum(l_scr[...], axis=1, keepdims=True), acc_scr[...]


def _fox_kernel(q_ref, k_ref, v_ref, cs_ref, eq_ref, ek_ref, o_ref, diag_ref, kaug_ref, *scratch,
                tq):
    kaug_ref[:, 0:LANES] = k_ref[...]
    kaug_ref[:, LANES:] = jnp.dot(cs_ref[...], ek_ref[0], preferred_element_type=F32).astype(BF16)
    row = lax.broadcasted_iota(jnp.int32, (tq, tq), 0)
    col = lax.broadcasted_iota(jnp.int32, (tq, tq), 1)
    diag_ref[...] = jnp.where(col <= row, 0.0, MASKED)

    def query_tile(qi, carry):
        rows = pl.ds(pl.multiple_of(qi * tq, tq), tq)
        cs_q = cs_ref[rows, :]
        feats = [jnp.dot(cs_q, eq_ref[0, sub], preferred_element_type=F32).astype(BF16)
                 for sub in range(2)]
        qaug = _stack_queries(q_ref[rows, :], feats[0], feats[1])
        l, acc = _flash_pipeline(qaug, kaug_ref, v_ref, scratch, qi, tq, diag_ref)
        out = acc / l
        lane = lax.broadcasted_iota(jnp.int32, (tq, LANES), 1)
        o_ref[rows, :] = jnp.where(lane < HEAD_DIM, out[0:tq], out[tq:2 * tq]).astype(o_ref.dtype)
        return carry

    lax.fori_loop(0, q_ref.shape[0] // tq, query_tile, 0)


def _fox_feature_maps(n_pairs):
    one_row = 3 * LANES
    ek = np.zeros((n_pairs, 4 * LANES, LANES), np.float32)
    eq = np.zeros((n_pairs, 2, 4 * LANES, LANES), np.float32)
    for p in range(n_pairs):
        for sub in range(2):
            head, base = 2 * p + sub, 6 * sub
            for t in range(3):
                ek[p, one_row, base + t] = 1.0
                ek[p, t * LANES + head, base + 3 + t] = -1.0
                eq[p, sub, t * LANES + head, base + t] = 1.0
                eq[p, sub, one_row, base + 3 + t] = 1.0
    return jnp.asarray(eq, BF16), jnp.asarray(ek, BF16)


def _fox_attention(qkv, csplit, *, batch, seq):
    n, width3 = qkv.shape
    n_pairs = width3 // 3 // LANES
    tq = min(ATTN_TILE, seq)
    eq, ek = _fox_feature_maps(n_pairs)
    return pl.pallas_call(
        functools.partial(_fox_kernel, tq=tq),
        out_shape=jax.ShapeDtypeStruct((n, n_pairs * LANES), BF16),
        grid=(batch, n_pairs),
        in_specs=[pl.BlockSpec((seq, LANES), lambda b, h: (b, h)),
                  pl.BlockSpec((seq, LANES), lambda b, h: (b, n_pairs + h)),
                  pl.BlockSpec((seq, LANES), lambda b, h: (b, 2 * n_pairs + h)),
                  pl.BlockSpec((seq, csplit.shape[1]), lambda b, h: (b, 0)),
                  pl.BlockSpec((1,) + eq.shape[1:], lambda b, h: (h, 0, 0, 0)),
                  pl.BlockSpec((1,) + ek.shape[1:], lambda b, h: (h, 0, 0))],
        out_specs=pl.BlockSpec((seq, LANES), lambda b, h: (b, h)),
        scratch_shapes=[pltpu.VMEM((tq, tq), F32)] + _flash_scratch(tq, seq),
        compiler_params=_params(2),
        name="fox_attention",
    )(qkv, qkv, qkv, csplit, eq, ek)


def _diff_kernel(slopes_ref, q_ref, k_ref, v_ref, lam_ref, g_ref, o_ref, diag_ref, kaug_ref,
                 *scratch, tq, lam_init):
    h = pl.program_id(1)
    seq = k_ref.shape[0]
    slope2 = slopes_ref[h] * LOG2E

    def bias_lanes(n_rows, first, base, other, other_value):
        lane = lax.broadcasted_iota(jnp.int32, (n_rows, LANES), 1)
        pos = (lax.broadcasted_iota(jnp.int32, (n_rows, LANES), 0) + first).astype(F32) * slope2
        terms = [t.astype(F32) for t in _split3(pos)]
        feat = jnp.where((lane >= other) & (lane < other + 3), other_value, 0.0)
        for t, term in enumerate(terms):
            feat = jnp.where(lane == base + t, term, feat)
        return feat.astype(BF16)

    kaug_ref[:, 0:LANES] = k_ref[...]
    kaug_ref[:, LANES:] = bias_lanes(seq, 0, 0, 3, -1.0)
    row = lax.broadcasted_iota(jnp.int32, (tq, tq), 0)
    col = lax.broadcasted_iota(jnp.int32, (tq, tq), 1)
    ahead = jnp.maximum(col - row, 0).astype(F32)
    diag_ref[...] = jnp.where((col // CHUNK) <= (row // CHUNK), -2.0 * slope2 * ahead, MASKED)
    t1 = jnp.sum(lam_ref[0:1, :] * lam_ref[1:2, :], axis=-1, keepdims=True)
    t2 = jnp.sum(lam_ref[2:3, :] * lam_ref[3:4, :], axis=-1, keepdims=True)
    lam = jnp.exp(t1) - jnp.exp(t2) + lam_init

    def query_tile(qi, carry):
        rows = pl.ds(pl.multiple_of(qi * tq, tq), tq)
        qfeat = bias_lanes(tq, qi * tq, 3, 0, 1.0)
        qaug = _stack_queries(q_ref[rows, :], qfeat, qfeat)
        l, acc = _flash_pipeline(qaug, kaug_ref, v_ref, scratch, qi, tq, diag_ref)
        out = acc / l
        o = out[0:tq] - lam * out[tq:2 * tq]
        o = o * lax.rsqrt(jnp.mean(o * o, axis=-1, keepdims=True) + RMS_EPS)
        o_ref[rows, :] = (o * g_ref[...] * (1.0 - lam_init)).astype(o_ref.dtype)
        return carry

    lax.fori_loop(0, seq // tq, query_tile, 0)


def _diff_attention(h, slopes, lam_rows, subln_g, *, batch, seq, lam_init):
    n, width3 = h.shape
    n_heads = width3 // 3 // LANES
    tq = min(ATTN_TILE, seq)
    grid_spec = pltpu.PrefetchScalarGridSpec(
        num_scalar_prefetch=1,
        grid=(batch, n_heads),
        in_specs=[pl.BlockSpec((seq, LANES), lambda b, hd, s: (b, hd)),
                  pl.BlockSpec((seq, LANES), lambda b, hd, s: (b, n_heads + hd)),
                  pl.BlockSpec((seq, LANES), lambda b, hd, s: (b, 2 * n_heads + hd)),
                  pl.BlockSpec(lam_rows.shape, lambda b, hd, s: (0, 0)),
                  pl.BlockSpec(subln_g.shape, lambda b, hd, s: (0, 0))],
        out_specs=pl.BlockSpec((seq, LANES), lambda b, hd, s: (b, hd)),
        scratch_shapes=[pltpu.VMEM((tq, tq), F32)] + _flash_scratch(tq, seq),
    )
    return pl.pallas_call(
        functools.partial(_diff_kernel, tq=tq, lam_init=lam_init),
        out_shape=jax.ShapeDtypeStruct((n, n_heads * LANES), BF16),
        grid_spec=grid_spec,
        compiler_params=_params(2),
        name="diff_attention",
    )(slopes, h, h, h, lam_rows, subln_g)


def _route(lg):
    lane = lax.broadcasted_iota(jnp.int32, lg.shape, 1).astype(F32)
    low = jnp.float32(-3e38)
    is_g = lane < N_GROUPS
    gmax = jnp.max(jnp.where(is_g, lg, low), axis=-1, keepdims=True)
    gsel = jnp.min(jnp.where(is_g & (lg == gmax), lane, float(LANES)), axis=-1, keepdims=True)
    gsum = jnp.sum(jnp.where(is_g, jnp.exp(lg - gmax), 0.0), axis=-1, keepdims=True)
    gw = 1.0 / gsum
    lo = N_GROUPS + EXPERTS_PER_GROUP * gsel
    in_grp = (lane >= lo) & (lane < lo + EXPERTS_PER_GROUP)
    el = jnp.where(in_grp, lg, low)
    v1 = jnp.max(el, axis=-1, keepdims=True)
    i1 = jnp.min(jnp.where(in_grp & (lg == v1), lane, float(LANES)), axis=-1, keepdims=True)
    rest = in_grp & (lane != i1)
    v2 = jnp.max(jnp.where(rest, lg, low), axis=-1, keepdims=True)
    i2 = jnp.min(jnp.where(rest & (lg == v2), lane, float(LANES)), axis=-1, keepdims=True)
    t = jnp.exp(v2 - v1)
    w1 = gw / (1.0 + t)
    w2 = gw * (t / (1.0 + t))
    rec = jnp.where(lane == 0.0, i1 - N_GROUPS,
                    jnp.where(lane == 1.0, i2 - N_GROUPS,
                              jnp.where(lane == 2.0, w1, jnp.where(lane == 3.0, w2, 0.0))))
    return rec


def _mix_ln_route(mix, x_ref, lng_ref, lnb_ref, wr_ref, br_ref, xo_ref, rec_ref, alpha):
    xo = _layer_norm(alpha * x_ref[...] + mix, lng_ref[...], lnb_ref[...])
    xo_ref[...] = xo
    d = xo.shape[1]
    x_hi = xo.astype(BF16)
    x_lo = (xo - x_hi.astype(F32)).astype(BF16)
    lg = (jnp.dot(x_hi, wr_ref[0:d, :], preferred_element_type=F32)
          + jnp.dot(x_lo, wr_ref[0:d, :], preferred_element_type=F32)
          + jnp.dot(x_hi, wr_ref[d:2 * d, :], preferred_element_type=F32)) + br_ref[...]
    rec_ref[...] = _route(lg)


def _outproj_even_kernel(gates_ref, attn_ref, x_ref, wout_ref, convw_ref, lng_ref, lnb_ref,
                         wr_ref, br_ref, xo_ref, rec_ref, tail_ref, *, tiles_per_seq, alpha):
    i = pl.program_id(0)
    tm = x_ref.shape[0]
    cw = gates_ref.shape[1] // 3
    gb = gates_ref[:, 0:cw]
    u = gates_ref[:, cw:2 * cw] * gates_ref[:, 2 * cw:3 * cw]

    @pl.when(i % tiles_per_seq == 0)
    def _():
        tail_ref[...] = jnp.zeros_like(tail_ref)

    tail = tail_ref[...]
    row = lax.broadcasted_iota(jnp.int32, (tm, 1), 0)
    u1 = jnp.where(row == 0, tail[7:8, :], pltpu.roll(u, 1, axis=0))
    u2 = jnp.where(row == 0, tail[6:7, :],
                   jnp.where(row == 1, tail[7:8, :], pltpu.roll(u, 2, axis=0)))
    tail_ref[...] = u[tm - 8:tm, :]
    y = u2 * convw_ref[0:1, :] + u1 * convw_ref[1:2, :] + u * convw_ref[2:3, :]
    a = (gb * y).astype(BF16)
    mix = (jnp.dot(a, wout_ref[0:cw, :], preferred_element_type=F32)
           + jnp.dot(attn_ref[...], wout_ref[cw:, :], preferred_element_type=F32))
    _mix_ln_route(mix, x_ref, lng_ref, lnb_ref, wr_ref, br_ref, xo_ref, rec_ref, alpha)


def _outproj_odd_kernel(attn_ref, x_ref, wout_ref, lng_ref, lnb_ref, wr_ref, br_ref,
                        xo_ref, rec_ref, *, alpha):
    mix = jnp.dot(attn_ref[...], wout_ref[...], preferred_element_type=F32)
    _mix_ln_route(mix, x_ref, lng_ref, lnb_ref, wr_ref, br_ref, xo_ref, rec_ref, alpha)


def _outproj(acts, xf, wout, convw, lng, lnb, wr, br, *, seq, alpha):
    n, d = xf.shape
    tm = min(ROW_TILE, seq)
    row_spec = lambda width: pl.BlockSpec((tm, width), lambda i: (i, 0))
    full = lambda a: pl.BlockSpec(a.shape, lambda i: (0, 0))
    even = convw is not None
    if even:
        body = functools.partial(_outproj_even_kernel, tiles_per_seq=seq // tm, alpha=alpha)
        operands = (*acts, xf, wout, convw, lng, lnb, wr, br)
        scratch = [pltpu.VMEM((8, acts[0].shape[1] // 3), F32)]
    else:
        body = functools.partial(_outproj_odd_kernel, alpha=alpha)
        operands = (*acts, xf, wout, lng, lnb, wr, br)
        scratch = []
    in_specs = [row_spec(a.shape[1]) for a in acts] + [row_spec(d)]
    in_specs += [full(a) for a in operands[len(acts) + 1:]]
    return pl.pallas_call(
        body,
        out_shape=(jax.ShapeDtypeStruct((n, d), F32), jax.ShapeDtypeStruct((n, LANES), F32)),
        grid=(n // tm,),
        in_specs=in_specs,
        out_specs=(row_spec(d), row_spec(LANES)),
        scratch_shapes=scratch,
        compiler_params=_params(1),
        name="outproj_even" if even else "outproj_odd",
    )(*operands)


def _plan_kernel(rec_ref, start_ref, pos_ref, run_ref):
    i = pl.program_id(0)

    @pl.when(i == 0)
    def _():
        run_ref[...] = jnp.zeros_like(run_ref)

    rec = rec_ref[...]
    tm = rec.shape[0]
    lane = lax.broadcasted_iota(jnp.int32, rec.shape, 1).astype(F32)
    oh1 = lane == rec[:, 0:1]
    oh2 = lane == rec[:, 1:2]
    oh = jnp.where(oh1 | oh2, 1.0, 0.0)
    r = lax.broadcasted_iota(jnp.int32, (tm, tm), 0)
    c = lax.broadcasted_iota(jnp.int32, (tm, tm), 1)
    earlier = jnp.where(c < r, 1.0, 0.0).astype(BF16)
    base = (jnp.dot(earlier, oh.astype(BF16), preferred_element_type=F32)
            + run_ref[0:1, :] + start_ref[...])
    p1 = jnp.sum(jnp.where(oh1, base, 0.0), axis=1, keepdims=True)
    p2 = jnp.sum(jnp.where(oh2, base, 0.0), axis=1, keepdims=True)
    run_ref[...] = run_ref[...] + jnp.sum(oh, axis=0, keepdims=True)
    pos_ref[...] = jnp.where(lane == 0.0, p1, jnp.where(lane == 1.0, p2, 0.0)).astype(jnp.int32)


def _plan_rows(rec, seg_start):
    n = rec.shape[0]
    tm = ROW_TILE
    return pl.pallas_call(
        _plan_kernel,
        out_shape=jax.ShapeDtypeStruct((n, LANES), jnp.int32),
        grid=(n // tm,),
        in_specs=[pl.BlockSpec((tm, LANES), lambda i: (i, 0)),
                  pl.BlockSpec(seg_start.shape, lambda i: (0, 0))],
        out_specs=pl.BlockSpec((tm, LANES), lambda i: (i, 0)),
        scratch_shapes=[pltpu.VMEM((8, LANES), F32)],
        compiler_params=_params(1),
        name="moe_plan",
    )(rec, seg_start)


def _dispatch_plan(rec, n_tok, rows):
    eid = rec[:, 0:2].astype(jnp.int32).reshape(-1)
    counts = jnp.sum((eid[:, None] == jnp.arange(N_EXPERTS, dtype=jnp.int32)[None, :])
                     .astype(jnp.int32), axis=0)
    padded = (counts + rows - 1) // rows * rows
    pad_end = jnp.cumsum(padded)
    seg_start = jnp.pad((pad_end - padded).astype(F32), (0, LANES - N_EXPERTS))[None, :]
    pos = _plan_rows(rec, seg_start)[:, 0:2].reshape(-1)
    cap = 2 * n_tok + N_EXPERTS * rows
    n_blocks = cap // rows
    n_pairs = 2 * n_tok
    row_pair = jnp.full((cap,), -1, jnp.int32).at[pos].set(jnp.arange(n_pairs, dtype=jnp.int32))
    src = jnp.where(row_pair >= 0, row_pair >> 1, 0)
    look_ahead = GATHER_BUFFERS - 1
    src_ext = jnp.concatenate([src, jnp.zeros((look_ahead * rows,), jnp.int32)])
    src_ext = src_ext.reshape(n_blocks + look_ahead, 1, rows)
    tm = ROW_TILE
    n_tiles = n_tok // tm
    pos_ext = jnp.transpose(pos.reshape(n_tiles, tm, 2), (0, 2, 1)).reshape(n_tiles, 2 * tm)
    pos_ext = jnp.concatenate([pos_ext, jnp.zeros((1, 2 * tm), jnp.int32)])
    pos_ext = pos_ext.reshape(n_tiles + 1, 1, 2 * tm)
    blk_start = jnp.arange(n_blocks, dtype=jnp.int32) * rows
    blk_e = jnp.minimum(jnp.sum((pad_end[None, :] <= blk_start[:, None]).astype(jnp.int32), axis=1),
                        N_EXPERTS - 1)
    return src_ext, pos_ext, blk_e


def _start_rows(src_hbm, idx_ref, idx_base, dst_ref, slot, sem, n):
    n_sems = sem.shape[1]
    for r in range(n):
        pltpu.make_async_copy(src_hbm.at[pl.ds(idx_ref[0, 0, idx_base + r], 1)],
                              dst_ref.at[slot, pl.ds(r, 1)],
                              sem.at[slot, r % n_sems]).start(priority=r % 2)


def _wait_rows(src_hbm, dst_ref, slot, sem):
    n_sems = sem.shape[1]
    share = dst_ref.shape[1] // n_sems
    for k in range(n_sems):
        pltpu.make_async_copy(src_hbm.at[pl.ds(0, share)], dst_ref.at[slot, pl.ds(0, share)],
                              sem.at[slot, k]).wait()


def _expert_mlp_kernel(blk_e_ref, src0_ref, src1_ref, src2_ref, x_hbm, wg_ref, wu_ref, wd_ref,
                       out_ref, xbuf, wgu_s, wd_s, sem):
    i = pl.program_id(0)
    rows = xbuf.shape[1]
    dh = wg_ref.shape[3]
    slot = i % GATHER_BUFFERS
    ahead1 = (i + 1) % GATHER_BUFFERS
    ahead2 = (i + 2) % GATHER_BUFFERS

    @pl.when(i == 0)
    def _():
        _start_rows(x_hbm, src0_ref, 0, xbuf, 0, sem, rows)
        _start_rows(x_hbm, src1_ref, 0, xbuf, 1, sem, rows)

    _wait_rows(x_hbm, xbuf, slot, sem)
    changed = jnp.logical_or(i == 0, blk_e_ref[i] != blk_e_ref[jnp.maximum(i - 1, 0)])

    @pl.when(changed)
    def _():
        wgu_s[:, 0:dh] = wg_ref[0, 0].astype(BF16)
        wgu_s[:, dh:2 * dh] = wu_ref[0, 0].astype(BF16)
        wd_s[...] = wd_ref[0, 0].astype(BF16)

    x = xbuf[slot].astype(BF16)
    _start_rows(x_hbm, src2_ref, 0, xbuf, ahead2, sem, rows)
    gu = jnp.dot(x, wgu_s[...], preferred_element_type=F32)
    g = gu[:, 0:dh]
    hid = (g / (1.0 + jnp.exp(-g))) * gu[:, dh:2 * dh]
    out_ref[...] = jnp.dot(hid.astype(BF16), wd_s[...], preferred_element_type=F32)

    @pl.when(i == pl.num_programs(0) - 1)
    def _():
        _wait_rows(x_hbm, xbuf, ahead1, sem)
        _wait_rows(x_hbm, xbuf, ahead2, sem)


def _expert_mlp(xf, w_gate, w_up, w_down, layer, src_ext, blk_e, rows):
    n_tok, d = xf.shape
    dh = w_gate.shape[3]
    n_blocks = blk_e.shape[0]
    idx_spec = lambda shift: pl.BlockSpec((1, 1, rows), lambda i, be: (i + shift, 0, 0),
                                          memory_space=pltpu.SMEM)
    grid_spec = pltpu.PrefetchScalarGridSpec(
        num_scalar_prefetch=1,
        grid=(n_blocks,),
        in_specs=[idx_spec(0), idx_spec(1), idx_spec(2),
                  pl.BlockSpec(memory_space=pl.ANY),
                  pl.BlockSpec((1, 1, d, dh), lambda i, be: (layer, be[i], 0, 0)),
                  pl.BlockSpec((1, 1, d, dh), lambda i, be: (layer, be[i], 0, 0)),
                  pl.BlockSpec((1, 1, dh, d), lambda i, be: (layer, be[i], 0, 0))],
        out_specs=pl.BlockSpec((rows, d), lambda i, be: (i, 0)),
        scratch_shapes=[pltpu.VMEM((GATHER_BUFFERS, rows, d), F32),
                        pltpu.VMEM((d, 2 * dh), BF16), pltpu.VMEM((dh, d), BF16),
                        pltpu.SemaphoreType.DMA((GATHER_BUFFERS, ROW_SEMS))],
    )
    return pl.pallas_call(
        _expert_mlp_kernel,
        out_shape=jax.ShapeDtypeStruct((n_blocks * rows, d), F32),
        grid_spec=grid_spec,
        compiler_params=_params(1),
        name="moe_experts",
    )(blk_e, src_ext, src_ext, src_ext, xf, w_gate, w_up, w_down)


def _combine_ln_kernel(pos0_ref, pos1_ref, yb_hbm, rec_ref, x_ref, lng_ref, lnb_ref, *refs,
                       alpha, next_mixer, tiles_per_seq):
    n_w = {"even": 3, "odd": 1, None: 0}[next_mixer]
    n_out = {"even": 3, "odd": 1, None: 0}[next_mixer]
    w_refs = refs[:n_w]
    xo_ref = refs[n_w]
    proj_refs = refs[n_w + 1:n_w + 1 + n_out]
    ya_buf, yb_buf, sem_a, sem_b = refs[n_w + 1 + n_out:n_w + 5 + n_out]
    i = pl.program_id(0)
    tm = x_ref.shape[0]
    slot = i % 2

    @pl.when(i == 0)
    def _():
        _start_rows(yb_hbm, pos0_ref, 0, ya_buf, 0, sem_a, tm)
        _start_rows(yb_hbm, pos0_ref, tm, yb_buf, 0, sem_b, tm)

    _wait_rows(yb_hbm, ya_buf, slot, sem_a)
    _wait_rows(yb_hbm, yb_buf, slot, sem_b)
    z = alpha * x_ref[...] + (rec_ref[:, 2:3] * ya_buf[slot] + rec_ref[:, 3:4] * yb_buf[slot])
    _start_rows(yb_hbm, pos1_ref, 0, ya_buf, 1 - slot, sem_a, tm)
    _start_rows(yb_hbm, pos1_ref, tm, yb_buf, 1 - slot, sem_b, tm)
    xo = _layer_norm(z, lng_ref[...], lnb_ref[...])
    xo_ref[...] = xo
    if next_mixer == "even":
        _project_even(xo.astype(BF16), *w_refs, *proj_refs, refs[-1], tiles_per_seq)
    elif next_mixer == "odd":
        _project_odd(xo.astype(BF16), *w_refs, *proj_refs)

    @pl.when(i == pl.num_programs(0) - 1)
    def _():
        _wait_rows(yb_hbm, ya_buf, 1 - slot, sem_a)
        _wait_rows(yb_hbm, yb_buf, 1 - slot, sem_b)


def _combine_ln(yb, pos_ext, rec, xf, lng, lnb, next_mixer, next_weights, *, seq, alpha):
    n, d = xf.shape
    tm = min(ROW_TILE, seq)
    idx_spec = lambda shift: pl.BlockSpec((1, 1, 2 * tm), lambda i: (i + shift, 0, 0),
                                          memory_space=pltpu.SMEM)
    outs = [(n, d, F32)]
    scratch = [pltpu.VMEM((2, tm, d), F32), pltpu.VMEM((2, tm, d), F32),
               pltpu.SemaphoreType.DMA((2, 1)), pltpu.SemaphoreType.DMA((2, 1))]
    if next_mixer == "even":
        outs += _even_projection_shapes(n, d, next_weights[0])
        scratch.append(pltpu.VMEM((8, LANES), F32))
    elif next_mixer == "odd":
        outs.append((n, next_weights[0].shape[1], BF16))
    return pl.pallas_call(
        functools.partial(_combine_ln_kernel, alpha=alpha, next_mixer=next_mixer,
                          tiles_per_seq=seq // tm),
        out_shape=tuple(jax.ShapeDtypeStruct((r, c), t) for r, c, t in outs),
        grid=(n // tm,),
        in_specs=[idx_spec(0), idx_spec(1),
                  pl.BlockSpec(memory_space=pl.ANY),
                  pl.BlockSpec((tm, LANES), lambda i: (i, 0)),
                  pl.BlockSpec((tm, d), lambda i: (i, 0)),
                  pl.BlockSpec(lng.shape, lambda i: (0, 0)),
                  pl.BlockSpec(lnb.shape, lambda i: (0, 0))]
                 + [pl.BlockSpec(w.shape, lambda i: (0, 0)) for w in next_weights],
        out_specs=tuple(pl.BlockSpec((tm, c), lambda i: (i, 0)) for _, c, _ in outs),
        scratch_shapes=scratch,
        compiler_params=_params(1),
        name="moe_combine_ln",
    )(pos_ext, pos_ext, yb, rec, xf, lng, lnb, *next_weights)


def kernel(x, ab_w_in, ab_b_forget, ab_conv_w, ab_w_out, c_w_in, c_lam_q1, c_lam_k1, c_lam_q2,
           c_lam_k2, c_subln_g, c_w_out, ln_mix_g, ln_mix_b, ln_ffn_g, ln_ffn_b, moe_w_group,
           moe_b_group, moe_w_expert, moe_b_expert, moe_w_gate, moe_w_up, moe_w_down):
    batch, seq, d = x.shape
    depth = ln_mix_g.shape[0]
    n_tok = batch * seq
    alpha = (2.0 * depth) ** 0.25
    q_scale = HEAD_DIM ** -0.5 * LOG2E
    conv_ch = ab_conv_w.shape[1]
    fox_heads = ab_b_forget.shape[1]
    n_diff_heads = d // (2 * HEAD_DIM)
    slopes = jnp.asarray(2.0 ** (-8.0 * np.arange(1, n_diff_heads + 1) / n_diff_heads), F32)
    xf = x.reshape(n_tok, d)

    def pad_lanes(v):
        return jnp.pad(v, (0, LANES - v.shape[0]))

    def projection_weights(layer):
        i = layer // 2
        if layer % 2 == 0:
            n_main = ab_w_in.shape[2] - fox_heads
            q_lo = 3 * conv_ch
            col_scale = np.ones((n_main,), np.float32)
            col_scale[q_lo:q_lo + fox_heads * HEAD_DIM] = q_scale
            w_main = (ab_w_in[i, :, :n_main] * col_scale).astype(BF16)
            wf = jnp.pad(ab_w_in[i, :, n_main:], ((0, 0), (0, LANES - fox_heads))).astype(BF16)
            return "even", (w_main, wf, pad_lanes(ab_b_forget[i])[None, :])
        col_scale = np.ones((c_w_in.shape[2],), np.float32)
        col_scale[0:d] = q_scale
        return "odd", ((c_w_in[i] * col_scale).astype(BF16),)

    projected = _inproj_even(xf, *projection_weights(0)[1], seq=seq)
    for layer in range(depth):
        i = layer // 2
        w_router = jnp.pad(jnp.concatenate([moe_w_group[layer], moe_w_expert[layer]], axis=1),
                           ((0, 0), (0, LANES - N_GROUPS - N_EXPERTS)))
        w_router_hi = w_router.astype(BF16)
        w_router = jnp.concatenate(
            [w_router_hi, (w_router - w_router_hi.astype(F32)).astype(BF16)], axis=0)
        b_router = pad_lanes(jnp.concatenate([moe_b_group[layer], moe_b_expert[layer]]))[None, :]
        lng, lnb = ln_mix_g[layer][None, :], ln_mix_b[layer][None, :]
        if layer % 2 == 0:
            gates, qkv, csplit = projected
            attn = _fox_attention(qkv, csplit, batch=batch, seq=seq)
            convw = jnp.pad(jnp.transpose(ab_conv_w[i]), ((0, 8 - ab_conv_w.shape[2]), (0, 0)))
            xf, rec = _outproj((gates, attn), xf, ab_w_out[i].astype(BF16), convw, lng, lnb,
                               w_router, b_router, seq=seq, alpha=alpha)
        else:
            (h,) = projected
            lam_init = 0.8 - 0.6 * math.exp(-0.3 * layer)
            lam_rows = jnp.pad(jnp.stack([pad_lanes(c_lam_q1[i]), pad_lanes(c_lam_k1[i]),
                                          pad_lanes(c_lam_q2[i]), pad_lanes(c_lam_k2[i])]),
                               ((0, 4), (0, 0)))
            attn = _diff_attention(h, slopes, lam_rows, c_subln_g[i][None, :],
                                   batch=batch, seq=seq, lam_init=lam_init)
            xf, rec = _outproj((attn,), xf, c_w_out[i].astype(BF16), None, lng, lnb,
                               w_router, b_router, seq=seq, alpha=alpha)

        src_ext, pos_ext, blk_e = _dispatch_plan(rec, n_tok, MOE_ROWS)
        yb = _expert_mlp(xf, moe_w_gate, moe_w_up, moe_w_down, layer, src_ext, blk_e, MOE_ROWS)
        next_mixer, next_weights = projection_weights(layer + 1) if layer + 1 < depth else (None, ())
        xf, *projected = _combine_ln(yb, pos_ext, rec, xf, ln_ffn_g[layer][None, :],
                                     ln_ffn_b[layer][None, :], next_mixer, next_weights,
                                     seq=seq, alpha=alpha)
    return xf.reshape(batch, seq, d)
```

```python
import functools
import math

import numpy as np
import jax
import jax.numpy as jnp
from jax import lax
from jax.experimental import pallas as pl
from jax.experimental.pallas import tpu as pltpu

F32 = jnp.float32
BF16 = jnp.bfloat16

LANES = 128
HEAD_DIM = 64
N_GROUPS = 4
EXPERTS_PER_GROUP = 8
N_EXPERTS = N_GROUPS * EXPERTS_PER_GROUP
CHUNK = 64
LN_EPS = 1e-5
RMS_EPS = 1e-5
LOG2E = math.log2(math.e)
M_INIT = -1e30
MASKED = -3e38
VMEM_LIMIT = 48 * 1024 * 1024

ROW_TILE = 256
PROJ_TILE = 512
ATTN_TILE = 256
SOFTMAX_ROWS = 32
MOE_ROWS = 256
GATHER_BUFFERS = 3
ROW_SEMS = 4
TOKEN_SLABS = 8
COL_CHUNK = 512

_NT = (((1,), (1,)), ((), ()))


def _params(n_grid):
    return pltpu.CompilerParams(dimension_semantics=("arbitrary",) * n_grid,
                                vmem_limit_bytes=VMEM_LIMIT)


def _layer_norm(z, g, b):
    mu = jnp.mean(z, axis=-1, keepdims=True)
    zc = z - mu
    var = jnp.mean(zc * zc, axis=-1, keepdims=True)
    return zc * lax.rsqrt(var + LN_EPS) * g + b


def _split3(v):
    a0 = v.astype(BF16)
    r1 = v - a0.astype(F32)
    a1 = r1.astype(BF16)
    a2 = (r1 - a1.astype(F32)).astype(BF16)
    return a0, a1, a2


def _project_even(x, w_ref, wf_ref, bf_ref, gates_ref, qkv_ref, csplit_ref, carry_ref,
                  tiles_per_seq):
    i = pl.program_id(0)
    n_gate = gates_ref.shape[1]
    for j in range(n_gate // COL_CHUNK):
        sl = slice(j * COL_CHUNK, (j + 1) * COL_CHUNK)
        gates_ref[:, sl] = jnp.dot(x, w_ref[:, sl], preferred_element_type=F32)
    for j in range(qkv_ref.shape[1] // COL_CHUNK):
        sl = slice(j * COL_CHUNK, (j + 1) * COL_CHUNK)
        wsl = slice(n_gate + j * COL_CHUNK, n_gate + (j + 1) * COL_CHUNK)
        qkv_ref[:, sl] = jnp.dot(x, w_ref[:, wsl], preferred_element_type=F32).astype(BF16)

    z = jnp.dot(x, wf_ref[...], preferred_element_type=F32) + bf_ref[...]
    lf = jnp.minimum(z, 0.0) - jnp.log1p(jnp.exp(-jnp.abs(z)))
    tm = lf.shape[0]
    r = lax.broadcasted_iota(jnp.int32, (tm, tm), 0)
    c = lax.broadcasted_iota(jnp.int32, (tm, tm), 1)
    tri = jnp.where(c <= r, 1.0, 0.0).astype(BF16)
    a0, a1, a2 = _split3(lf)
    cs = (jnp.dot(tri, a0, preferred_element_type=F32)
          + jnp.dot(tri, a1, preferred_element_type=F32)
          + jnp.dot(tri, a2, preferred_element_type=F32))

    @pl.when(i % tiles_per_seq == 0)
    def _():
        carry_ref[...] = jnp.zeros_like(carry_ref)

    cs = cs + carry_ref[0:1, :]
    carry_ref[...] = jnp.broadcast_to(cs[tm - 1:tm, :], carry_ref.shape)
    c0, c1, c2 = _split3(cs * LOG2E)
    lane = lax.broadcasted_iota(jnp.int32, (tm, LANES), 1)
    csplit_ref[:, 0:LANES] = c0
    csplit_ref[:, LANES:2 * LANES] = c1
    csplit_ref[:, 2 * LANES:3 * LANES] = c2
    csplit_ref[:, 3 * LANES:4 * LANES] = jnp.where(lane == 0, 1.0, 0.0).astype(BF16)


def _project_odd(x, w_ref, h_ref):
    for j in range(h_ref.shape[1] // COL_CHUNK):
        sl = slice(j * COL_CHUNK, (j + 1) * COL_CHUNK)
        h_ref[:, sl] = jnp.dot(x, w_ref[:, sl], preferred_element_type=F32).astype(BF16)


def _even_projection_shapes(n, d, w_main):
    n_gate = 3 * (d // 2)
    return [(n, n_gate, F32), (n, w_main.shape[1] - n_gate, BF16), (n, 4 * LANES, BF16)]


def _inproj_even_kernel(x_ref, w_ref, wf_ref, bf_ref, gates_ref, qkv_ref, csplit_ref, carry_ref,
                        *, tiles_per_seq):
    _project_even(x_ref[...].astype(BF16), w_ref, wf_ref, bf_ref, gates_ref, qkv_ref, csplit_ref,
                  carry_ref, tiles_per_seq)


def _inproj_even(xf, w_main, wf, bf, *, seq):
    n, d = xf.shape
    tm = min(PROJ_TILE, seq)
    outs = _even_projection_shapes(n, d, w_main)
    return pl.pallas_call(
        functools.partial(_inproj_even_kernel, tiles_per_seq=seq // tm),
        out_shape=tuple(jax.ShapeDtypeStruct((r, c), t) for r, c, t in outs),
        grid=(n // tm,),
        in_specs=[pl.BlockSpec((tm, d), lambda i: (i, 0)),
                  pl.BlockSpec(w_main.shape, lambda i: (0, 0)),
                  pl.BlockSpec(wf.shape, lambda i: (0, 0)),
                  pl.BlockSpec(bf.shape, lambda i: (0, 0))],
        out_specs=tuple(pl.BlockSpec((tm, c), lambda i: (i, 0)) for _, c, _ in outs),
        scratch_shapes=[pltpu.VMEM((8, LANES), F32)],
        compiler_params=_params(1),
        name="inproj_even",
    )(xf, w_main, wf, bf)


def _flash_scratch(tq, seq):
    r2 = 2 * tq
    return ([pltpu.VMEM((seq, 2 * LANES), BF16)]
            + [pltpu.VMEM((r2, tq), F32) for _ in range(2)]
            + [pltpu.VMEM((r2, tq), BF16) for _ in range(2)]
            + [pltpu.VMEM((r2, LANES), F32) for _ in range(2)]
            + [pltpu.VMEM((r2, LANES), F32) for _ in range(3)])


def _stack_queries(q, feat_a, feat_b):
    lane = lax.broadcasted_iota(jnp.int32, q.shape, 1)
    zero = jnp.zeros_like(q)
    qa = jnp.concatenate([jnp.where(lane < HEAD_DIM, q, zero), feat_a], axis=1)
    qb = jnp.concatenate([jnp.where(lane >= HEAD_DIM, q, zero), feat_b], axis=1)
    return jnp.concatenate([qa, qb], axis=0)


def _flash_pipeline(qaug, kaug_ref, v_ref, scratch, qi, tq, diag_bias_ref):
    s_x, s_y, p_x, p_y, al_x, al_y, m_scr, l_scr, acc_scr = scratch
    r2 = 2 * tq
    n_rep = tq // LANES

    def scores(j, s_ref):
        off = pl.multiple_of(j * tq, tq)
        s_ref[...] = lax.dot_general(qaug, kaug_ref[pl.ds(off, tq), :], _NT,
                                     preferred_element_type=F32)

    def weighted_values(j, p_ref, al_ref):
        off = pl.multiple_of(j * tq, tq)
        acc_scr[...] = al_ref[...] * acc_scr[...] + jnp.dot(
            p_ref[...], v_ref[pl.ds(off, tq), :], preferred_element_type=F32)

    def softmax(s_ref, p_ref, al_ref, bias_fn):
        for c in range(r2 // SOFTMAX_ROWS):
            rows = slice(c * SOFTMAX_ROWS, (c + 1) * SOFTMAX_ROWS)
            x = s_ref[rows, :]
            if bias_fn is not None:
                x = x + bias_fn(c)
            m_old = m_scr[rows, :]
            m_new = jnp.maximum(m_old, jnp.max(x, axis=1, keepdims=True))
            alpha = jnp.exp2(m_old - m_new)
            p = jnp.exp2(x - jnp.concatenate([m_new] * n_rep, axis=1))
            p_sum = p[:, 0:LANES]
            for t in range(1, n_rep):
                p_sum = p_sum + p[:, t * LANES:(t + 1) * LANES]
            m_scr[rows, :] = m_new
            l_scr[rows, :] = alpha * l_scr[rows, :] + p_sum
            al_ref[rows, :] = alpha
            p_ref[rows, :] = p.astype(BF16)

    def diag_bias(c):
        r0 = (c * SOFTMAX_ROWS) % tq
        return diag_bias_ref[r0:r0 + SOFTMAX_ROWS, :]

    scores(0, s_x)
    m_scr[...] = jnp.full(m_scr.shape, M_INIT, F32)
    l_scr[...] = jnp.zeros(l_scr.shape, F32)
    acc_scr[...] = jnp.zeros(acc_scr.shape, F32)
    p_y[...] = jnp.zeros(p_y.shape, BF16)
    al_y[...] = jnp.ones(al_y.shape, F32)

    def pair(i, carry):
        e = 2 * i
        weighted_values(jnp.maximum(e - 1, 0), p_y, al_y)
        scores(e + 1, s_y)
        softmax(s_x, p_x, al_x, None)
        weighted_values(e, p_x, al_x)
        scores(e + 2, s_x)
        softmax(s_y, p_y, al_y, None)
        return carry

    lax.fori_loop(0, qi // 2, pair, 0)

    e = 2 * (qi // 2)
    gate = jnp.where(qi % 2 == 1, 0.0, MASKED).astype(F32)
    weighted_values(jnp.maximum(e - 1, 0), p_y, al_y)
    scores(qi, s_y)
    softmax(s_x, p_x, al_x, lambda c: gate)
    weighted_values(e, p_x, al_x)
    softmax(s_y, p_y, al_y, diag_bias)
    weighted_values(qi, p_y, al_y)
    return jnp.sum(l_scr[...], axis=1, keepdims=True), acc_scr[...]


def _fox_kernel(q_ref, k_ref, v_ref, cs_ref, eq_ref, ek_ref, o_ref, diag_ref, kaug_ref, *scratch,
                tq):
    kaug_ref[:, 0:LANES] = k_ref[...]
    kaug_ref[:, LANES:] = jnp.dot(cs_ref[...], ek_ref[0], preferred_element_type=F32).astype(BF16)
    row = lax.broadcasted_iota(jnp.int32, (tq, tq), 0)
    col = lax.broadcasted_iota(jnp.int32, (tq, tq), 1)
    diag_ref[...] = jnp.where(col <= row, 0.0, MASKED)

    def query_tile(qi, carry):
        rows = pl.ds(pl.multiple_of(qi * tq, tq), tq)
        cs_q = cs_ref[rows, :]
        feats = [jnp.dot(cs_q, eq_ref[0, sub], preferred_element_type=F32).astype(BF16)
                 for sub in range(2)]
        qaug = _stack_queries(q_ref[rows, :], feats[0], feats[1])
        l, acc = _flash_pipeline(qaug, kaug_ref, v_ref, scratch, qi, tq, diag_ref)
        out = acc / l
        lane = lax.broadcasted_iota(jnp.int32, (tq, LANES), 1)
        o_ref[rows, :] = jnp.where(lane < HEAD_DIM, out[0:tq], out[tq:2 * tq]).astype(o_ref.dtype)
        return carry

    lax.fori_loop(0, q_ref.shape[0] // tq, query_tile, 0)


def _fox_feature_maps(n_pairs):
    one_row = 3 * LANES
    ek = np.zeros((n_pairs, 4 * LANES, LANES), np.float32)
    eq = np.zeros((n_pairs, 2, 4 * LANES, LANES), np.float32)
    for p in range(n_pairs):
        for sub in range(2):
            head, base = 2 * p + sub, 6 * sub
            for t in range(3):
                ek[p, one_row, base + t] = 1.0
                ek[p, t * LANES + head, base + 3 + t] = -1.0
                eq[p, sub, t * LANES + head, base + t] = 1.0
                eq[p, sub, one_row, base + 3 + t] = 1.0
    return jnp.asarray(eq, BF16), jnp.asarray(ek, BF16)


def _fox_attention(qkv, csplit, *, batch, seq):
    n, width3 = qkv.shape
    n_pairs = width3 // 3 // LANES
    tq = min(ATTN_TILE, seq)
    eq, ek = _fox_feature_maps(n_pairs)
    return pl.pallas_call(
        functools.partial(_fox_kernel, tq=tq),
        out_shape=jax.ShapeDtypeStruct((n, n_pairs * LANES), BF16),
        grid=(batch, n_pairs),
        in_specs=[pl.BlockSpec((seq, LANES), lambda b, h: (b, h)),
                  pl.BlockSpec((seq, LANES), lambda b, h: (b, n_pairs + h)),
                  pl.BlockSpec((seq, LANES), lambda b, h: (b, 2 * n_pairs + h)),
                  pl.BlockSpec((seq, csplit.shape[1]), lambda b, h: (b, 0)),
                  pl.BlockSpec((1,) + eq.shape[1:], lambda b, h: (h, 0, 0, 0)),
                  pl.BlockSpec((1,) + ek.shape[1:], lambda b, h: (h, 0, 0))],
        out_specs=pl.BlockSpec((seq, LANES), lambda b, h: (b, h)),
        scratch_shapes=[pltpu.VMEM((tq, tq), F32)] + _flash_scratch(tq, seq),
        compiler_params=_params(2),
        name="fox_attention",
    )(qkv, qkv, qkv, csplit, eq, ek)


def _diff_kernel(slopes_ref, q_ref, k_ref, v_ref, lam_ref, g_ref, o_ref, diag_ref, kaug_ref,
                 *scratch, tq, lam_init):
    h = pl.program_id(1)
    seq = k_ref.shape[0]
    slope2 = slopes_ref[h] * LOG2E

    def bias_lanes(n_rows, first, base, other, other_value):
        lane = lax.broadcasted_iota(jnp.int32, (n_rows, LANES), 1)
        pos = (lax.broadcasted_iota(jnp.int32, (n_rows, LANES), 0) + first).astype(F32) * slope2
        terms = [t.astype(F32) for t in _split3(pos)]
        feat = jnp.where((lane >= other) & (lane < other + 3), other_value, 0.0)
        for t, term in enumerate(terms):
            feat = jnp.where(lane == base + t, term, feat)
        return feat.astype(BF16)

    kaug_ref[:, 0:LANES] = k_ref[...]
    kaug_ref[:, LANES:] = bias_lanes(seq, 0, 0, 3, -1.0)
    row = lax.broadcasted_iota(jnp.int32, (tq, tq), 0)
    col = lax.broadcasted_iota(jnp.int32, (tq, tq), 1)
    ahead = jnp.maximum(col - row, 0).astype(F32)
    diag_ref[...] = jnp.where((col // CHUNK) <= (row // CHUNK), -2.0 * slope2 * ahead, MASKED)
    t1 = jnp.sum(lam_ref[0:1, :] * lam_ref[1:2, :], axis=-1, keepdims=True)
    t2 = jnp.sum(lam_ref[2:3, :] * lam_ref[3:4, :], axis=-1, keepdims=True)
    lam = jnp.exp(t1) - jnp.exp(t2) + lam_init

    def query_tile(qi, carry):
        rows = pl.ds(pl.multiple_of(qi * tq, tq), tq)
        qfeat = bias_lanes(tq, qi * tq, 3, 0, 1.0)
        qaug = _stack_queries(q_ref[rows, :], qfeat, qfeat)
        l, acc = _flash_pipeline(qaug, kaug_ref, v_ref, scratch, qi, tq, diag_ref)
        out = acc / l
        o = out[0:tq] - lam * out[tq:2 * tq]
        o = o * lax.rsqrt(jnp.mean(o * o, axis=-1, keepdims=True) + RMS_EPS)
        o_ref[rows, :] = (o * g_ref[...] * (1.0 - lam_init)).astype(o_ref.dtype)
        return carry

    lax.fori_loop(0, seq // tq, query_tile, 0)


def _diff_attention(h, slopes, lam_rows, subln_g, *, batch, seq, lam_init):
    n, width3 = h.shape
    n_heads = width3 // 3 // LANES
    tq = min(ATTN_TILE, seq)
    grid_spec = pltpu.PrefetchScalarGridSpec(
        num_scalar_prefetch=1,
        grid=(batch, n_heads),
        in_specs=[pl.BlockSpec((seq, LANES), lambda b, hd, s: (b, hd)),
                  pl.BlockSpec((seq, LANES), lambda b, hd, s: (b, n_heads + hd)),
                  pl.BlockSpec((seq, LANES), lambda b, hd, s: (b, 2 * n_heads + hd)),
                  pl.BlockSpec(lam_rows.shape, lambda b, hd, s: (0, 0)),
                  pl.BlockSpec(subln_g.shape, lambda b, hd, s: (0, 0))],
        out_specs=pl.BlockSpec((seq, LANES), lambda b, hd, s: (b, hd)),
        scratch_shapes=[pltpu.VMEM((tq, tq), F32)] + _flash_scratch(tq, seq),
    )
    return pl.pallas_call(
        functools.partial(_diff_kernel, tq=tq, lam_init=lam_init),
        out_shape=jax.ShapeDtypeStruct((n, n_heads * LANES), BF16),
        grid_spec=grid_spec,
        compiler_params=_params(2),
        name="diff_attention",
    )(slopes, h, h, h, lam_rows, subln_g)


def _route(lg):
    lane = lax.broadcasted_iota(jnp.int32, lg.shape, 1).astype(F32)
    low = jnp.float32(-3e38)
    is_g = lane < N_GROUPS
    gmax = jnp.max(jnp.where(is_g, lg, low), axis=-1, keepdims=True)
    gsel = jnp.min(jnp.where(is_g & (lg == gmax), lane, float(LANES)), axis=-1, keepdims=True)
    gsum = jnp.sum(jnp.where(is_g, jnp.exp(lg - gmax), 0.0), axis=-1, keepdims=True)
    gw = 1.0 / gsum
    lo = N_GROUPS + EXPERTS_PER_GROUP * gsel
    in_grp = (lane >= lo) & (lane < lo + EXPERTS_PER_GROUP)
    el = jnp.where(in_grp, lg, low)
    v1 = jnp.max(el, axis=-1, keepdims=True)
    i1 = jnp.min(jnp.where(in_grp & (lg == v1), lane, float(LANES)), axis=-1, keepdims=True)
    rest = in_grp & (lane != i1)
    v2 = jnp.max(jnp.where(rest, lg, low), axis=-1, keepdims=True)
    i2 = jnp.min(jnp.where(rest & (lg == v2), lane, float(LANES)), axis=-1, keepdims=True)
    t = jnp.exp(v2 - v1)
    w1 = gw / (1.0 + t)
    w2 = gw * (t / (1.0 + t))
    rec = jnp.where(lane == 0.0, i1 - N_GROUPS,
                    jnp.where(lane == 1.0, i2 - N_GROUPS,
                              jnp.where(lane == 2.0, w1, jnp.where(lane == 3.0, w2, 0.0))))
    return rec


def _mix_ln_route(mix, x_ref, lng_ref, lnb_ref, wr_ref, br_ref, xo_ref, rec_ref, alpha):
    xo = _layer_norm(alpha * x_ref[...] + mix, lng_ref[...], lnb_ref[...])
    xo_ref[...] = xo
    d = xo.shape[1]
    x_hi = xo.astype(BF16)
    x_lo = (xo - x_hi.astype(F32)).astype(BF16)
    lg = (jnp.dot(x_hi, wr_ref[0:d, :], preferred_element_type=F32)
          + jnp.dot(x_lo, wr_ref[0:d, :], preferred_element_type=F32)
          + jnp.dot(x_hi, wr_ref[d:2 * d, :], preferred_element_type=F32)) + br_ref[...]
    rec_ref[...] = _route(lg)


def _outproj_even_kernel(gates_ref, attn_ref, x_ref, wout_ref, convw_ref, lng_ref, lnb_ref,
                         wr_ref, br_ref, xo_ref, rec_ref, tail_ref, *, tiles_per_seq, alpha):
    i = pl.program_id(0)
    tm = x_ref.shape[0]
    cw = gates_ref.shape[1] // 3
    gb = gates_ref[:, 0:cw]
    u = gates_ref[:, cw:2 * cw] * gates_ref[:, 2 * cw:3 * cw]

    @pl.when(i % tiles_per_seq == 0)
    def _():
        tail_ref[...] = jnp.zeros_like(tail_ref)

    tail = tail_ref[...]
    row = lax.broadcasted_iota(jnp.int32, (tm, 1), 0)
    u1 = jnp.where(row == 0, tail[7:8, :], pltpu.roll(u, 1, axis=0))
    u2 = jnp.where(row == 0, tail[6:7, :],
                   jnp.where(row == 1, tail[7:8, :], pltpu.roll(u, 2, axis=0)))
    tail_ref[...] = u[tm - 8:tm, :]
    y = u2 * convw_ref[0:1, :] + u1 * convw_ref[1:2, :] + u * convw_ref[2:3, :]
    a = (gb * y).astype(BF16)
    mix = (jnp.dot(a, wout_ref[0:cw, :], preferred_element_type=F32)
           + jnp.dot(attn_ref[...], wout_ref[cw:, :], preferred_element_type=F32))
    _mix_ln_route(mix, x_ref, lng_ref, lnb_ref, wr_ref, br_ref, xo_ref, rec_ref, alpha)


def _outproj_odd_kernel(attn_ref, x_ref, wout_ref, lng_ref, lnb_ref, wr_ref, br_ref,
                        xo_ref, rec_ref, *, alpha):
    mix = jnp.dot(attn_ref[...], wout_ref[...], preferred_element_type=F32)
    _mix_ln_route(mix, x_ref, lng_ref, lnb_ref, wr_ref, br_ref, xo_ref, rec_ref, alpha)


def _outproj(acts, xf, wout, convw, lng, lnb, wr, br, *, seq, alpha):
    n, d = xf.shape
    tm = min(ROW_TILE, seq)
    row_spec = lambda width: pl.BlockSpec((tm, width), lambda i: (i, 0))
    full = lambda a: pl.BlockSpec(a.shape, lambda i: (0, 0))
    even = convw is not None
    if even:
        body = functools.partial(_outproj_even_kernel, tiles_per_seq=seq // tm, alpha=alpha)
        operands = (*acts, xf, wout, convw, lng, lnb, wr, br)
        scratch = [pltpu.VMEM((8, acts[0].shape[1] // 3), F32)]
    else:
        body = functools.partial(_outproj_odd_kernel, alpha=alpha)
        operands = (*acts, xf, wout, lng, lnb, wr, br)
        scratch = []
    in_specs = [row_spec(a.shape[1]) for a in acts] + [row_spec(d)]
    in_specs += [full(a) for a in operands[len(acts) + 1:]]
    return pl.pallas_call(
        body,
        out_shape=(jax.ShapeDtypeStruct((n, d), F32), jax.ShapeDtypeStruct((n, LANES), F32)),
        grid=(n // tm,),
        in_specs=in_specs,
        out_specs=(row_spec(d), row_spec(LANES)),
        scratch_shapes=scratch,
        compiler_params=_params(1),
        name="outproj_even" if even else "outproj_odd",
    )(*operands)


def _plan_kernel(rec_ref, start_ref, pos_ref, run_ref):
    i = pl.program_id(0)

    @pl.when(i == 0)
    def _():
        run_ref[...] = jnp.zeros_like(run_ref)

    rec = rec_ref[...]
    tm = rec.shape[0]
    lane = lax.broadcasted_iota(jnp.int32, rec.shape, 1).astype(F32)
    oh1 = lane == rec[:, 0:1]
    oh2 = lane == rec[:, 1:2]
    oh = jnp.where(oh1 | oh2, 1.0, 0.0)
    r = lax.broadcasted_iota(jnp.int32, (tm, tm), 0)
    c = lax.broadcasted_iota(jnp.int32, (tm, tm), 1)
    earlier = jnp.where(c < r, 1.0, 0.0).astype(BF16)
    base = (jnp.dot(earlier, oh.astype(BF16), preferred_element_type=F32)
            + run_ref[0:1, :] + start_ref[...])
    p1 = jnp.sum(jnp.where(oh1, base, 0.0), axis=1, keepdims=True)
    p2 = jnp.sum(jnp.where(oh2, base, 0.0), axis=1, keepdims=True)
    run_ref[...] = run_ref[...] + jnp.sum(oh, axis=0, keepdims=True)
    pos_ref[...] = jnp.where(lane == 0.0, p1, jnp.where(lane == 1.0, p2, 0.0)).astype(jnp.int32)


def _plan_rows(rec, seg_start):
    n = rec.shape[0]
    tm = ROW_TILE
    return pl.pallas_call(
        _plan_kernel,
        out_shape=jax.ShapeDtypeStruct((n, LANES), jnp.int32),
        grid=(n // tm,),
        in_specs=[pl.BlockSpec((tm, LANES), lambda i: (i, 0)),
                  pl.BlockSpec(seg_start.shape, lambda i: (0, 0))],
        out_specs=pl.BlockSpec((tm, LANES), lambda i: (i, 0)),
        scratch_shapes=[pltpu.VMEM((8, LANES), F32)],
        compiler_params=_params(1),
        name="moe_plan",
    )(rec, seg_start)


def _dispatch_plan(rec, n_tok, rows):
    eid = rec[:, 0:2].astype(jnp.int32).reshape(-1)
    counts = jnp.sum((eid[:, None] == jnp.arange(N_EXPERTS, dtype=jnp.int32)[None, :])
                     .astype(jnp.int32), axis=0)
    padded = (counts + rows - 1) // rows * rows
    pad_end = jnp.cumsum(padded)
    seg_start = jnp.pad((pad_end - padded).astype(F32), (0, LANES - N_EXPERTS))[None, :]
    pos = _plan_rows(rec, seg_start)[:, 0:2].reshape(-1)
    cap = 2 * n_tok + N_EXPERTS * rows
    n_blocks = cap // rows
    n_pairs = 2 * n_tok
    row_pair = jnp.full((cap,), -1, jnp.int32).at[pos].set(jnp.arange(n_pairs, dtype=jnp.int32))
    src = jnp.where(row_pair >= 0, (row_pair >> 1) * TOKEN_SLABS, 0)
    look_ahead = GATHER_BUFFERS - 1
    src_ext = jnp.concatenate([src, jnp.zeros((look_ahead * rows,), jnp.int32)])
    src_ext = src_ext.reshape(n_blocks + look_ahead, 1, rows)
    tm = ROW_TILE
    n_tiles = n_tok // tm
    pos_ext = jnp.transpose((pos * TOKEN_SLABS).reshape(n_tiles, tm, 2), (0, 2, 1))
    pos_ext = pos_ext.reshape(n_tiles, 2 * tm)
    pos_ext = jnp.concatenate([pos_ext, jnp.zeros((1, 2 * tm), jnp.int32)])
    pos_ext = pos_ext.reshape(n_tiles + 1, 1, 2 * tm)
    blk_start = jnp.arange(n_blocks, dtype=jnp.int32) * rows
    blk_e = jnp.minimum(jnp.sum((pad_end[None, :] <= blk_start[:, None]).astype(jnp.int32), axis=1),
                        N_EXPERTS - 1)
    return src_ext, pos_ext, blk_e


def _start_rows(src_hbm, idx_ref, idx_base, dst_ref, slot, sem, n, unit=1):
    n_sems = sem.shape[1]
    for r in range(n):
        first = idx_ref[0, 0, idx_base + r]
        if unit > 1:
            first = pl.multiple_of(first, unit)
        pltpu.make_async_copy(src_hbm.at[pl.ds(first, unit)],
                              dst_ref.at[slot, pl.ds(r * unit, unit)],
                              sem.at[slot, r % n_sems]).start(priority=r % 2)


def _wait_rows(src_hbm, dst_ref, slot, sem):
    n_sems = sem.shape[1]
    share = dst_ref.shape[1] // n_sems
    for k in range(n_sems):
        pltpu.make_async_copy(src_hbm.at[pl.ds(0, share)], dst_ref.at[slot, pl.ds(0, share)],
                              sem.at[slot, k]).wait()


def _expert_mlp_kernel(blk_e_ref, src0_ref, src1_ref, src2_ref, x_hbm, wg_ref, wu_ref, wd_ref,
                       out_ref, xbuf, wgu_s, wd_s, sem):
    i = pl.program_id(0)
    rows = xbuf.shape[1] // TOKEN_SLABS
    dh = wg_ref.shape[3]
    slot = i % GATHER_BUFFERS
    ahead1 = (i + 1) % GATHER_BUFFERS
    ahead2 = (i + 2) % GATHER_BUFFERS

    @pl.when(i == 0)
    def _():
        _start_rows(x_hbm, src0_ref, 0, xbuf, 0, sem, rows, TOKEN_SLABS)
        _start_rows(x_hbm, src1_ref, 0, xbuf, 1, sem, rows, TOKEN_SLABS)

    _wait_rows(x_hbm, xbuf, slot, sem)
    changed = jnp.logical_or(i == 0, blk_e_ref[i] != blk_e_ref[jnp.maximum(i - 1, 0)])

    @pl.when(changed)
    def _():
        wgu_s[:, 0:dh] = wg_ref[0, 0].astype(BF16)
        wgu_s[:, dh:2 * dh] = wu_ref[0, 0].astype(BF16)
        wd_s[...] = wd_ref[0, 0].astype(BF16)

    x = jnp.concatenate([xbuf[slot, pl.ds(j, rows, stride=TOKEN_SLABS), :].astype(BF16)
                         for j in range(TOKEN_SLABS)], axis=1)
    _start_rows(x_hbm, src2_ref, 0, xbuf, ahead2, sem, rows, TOKEN_SLABS)
    gu = jnp.dot(x, wgu_s[...], preferred_element_type=F32)
    g = gu[:, 0:dh]
    hid = (g / (1.0 + jnp.exp(-g))) * gu[:, dh:2 * dh]
    y = jnp.dot(hid.astype(BF16), wd_s[...], preferred_element_type=F32)
    for j in range(TOKEN_SLABS):
        out_ref[pl.ds(j, rows, stride=TOKEN_SLABS), :] = y[:, j * LANES:(j + 1) * LANES]

    @pl.when(i == pl.num_programs(0) - 1)
    def _():
        _wait_rows(x_hbm, xbuf, ahead1, sem)
        _wait_rows(x_hbm, xbuf, ahead2, sem)


def _expert_mlp(xf, w_gate, w_up, w_down, layer, src_ext, blk_e, rows):
    n_tok, d = xf.shape
    dh = w_gate.shape[3]
    n_blocks = blk_e.shape[0]
    idx_spec = lambda shift: pl.BlockSpec((1, 1, rows), lambda i, be: (i + shift, 0, 0),
                                          memory_space=pltpu.SMEM)
    grid_spec = pltpu.PrefetchScalarGridSpec(
        num_scalar_prefetch=1,
        grid=(n_blocks,),
        in_specs=[idx_spec(0), idx_spec(1), idx_spec(2),
                  pl.BlockSpec(memory_space=pl.ANY),
                  pl.BlockSpec((1, 1, d, dh), lambda i, be: (layer, be[i], 0, 0)),
                  pl.BlockSpec((1, 1, d, dh), lambda i, be: (layer, be[i], 0, 0)),
                  pl.BlockSpec((1, 1, dh, d), lambda i, be: (layer, be[i], 0, 0))],
        out_specs=pl.BlockSpec((rows * TOKEN_SLABS, LANES), lambda i, be: (i, 0)),
        scratch_shapes=[pltpu.VMEM((GATHER_BUFFERS, rows * TOKEN_SLABS, LANES), F32),
                        pltpu.VMEM((d, 2 * dh), BF16), pltpu.VMEM((dh, d), BF16),
                        pltpu.SemaphoreType.DMA((GATHER_BUFFERS, ROW_SEMS))],
    )
    x_slabs = xf.reshape(n_tok * TOKEN_SLABS, LANES)
    return pl.pallas_call(
        _expert_mlp_kernel,
        out_shape=jax.ShapeDtypeStruct((n_blocks * rows * TOKEN_SLABS, LANES), F32),
        grid_spec=grid_spec,
        compiler_params=_params(1),
        name="moe_experts",
    )(blk_e, src_ext, src_ext, src_ext, x_slabs, w_gate, w_up, w_down)


def _combine_ln_kernel(pos0_ref, pos1_ref, yb_hbm, rec_ref, x_ref, lng_ref, lnb_ref, *refs,
                       alpha, next_mixer, tiles_per_seq):
    n_w = {"even": 3, "odd": 1, None: 0}[next_mixer]
    n_out = {"even": 3, "odd": 1, None: 0}[next_mixer]
    w_refs = refs[:n_w]
    xo_ref = refs[n_w]
    proj_refs = refs[n_w + 1:n_w + 1 + n_out]
    ya_buf, yb_buf, sem_a, sem_b = refs[n_w + 1 + n_out:n_w + 5 + n_out]
    i = pl.program_id(0)
    tm = x_ref.shape[0]
    slot = i % 2

    @pl.when(i == 0)
    def _():
        _start_rows(yb_hbm, pos0_ref, 0, ya_buf, 0, sem_a, tm, TOKEN_SLABS)
        _start_rows(yb_hbm, pos0_ref, tm, yb_buf, 0, sem_b, tm, TOKEN_SLABS)

    _wait_rows(yb_hbm, ya_buf, slot, sem_a)
    _wait_rows(yb_hbm, yb_buf, slot, sem_b)

    def token_rows(buf):
        return jnp.concatenate([buf[slot, pl.ds(j, tm, stride=TOKEN_SLABS), :]
                                for j in range(TOKEN_SLABS)], axis=1)

    z = alpha * x_ref[...] + (rec_ref[:, 2:3] * token_rows(ya_buf)
                              + rec_ref[:, 3:4] * token_rows(yb_buf))
    _start_rows(yb_hbm, pos1_ref, 0, ya_buf, 1 - slot, sem_a, tm, TOKEN_SLABS)
    _start_rows(yb_hbm, pos1_ref, tm, yb_buf, 1 - slot, sem_b, tm, TOKEN_SLABS)
    xo = _layer_norm(z, lng_ref[...], lnb_ref[...])
    xo_ref[...] = xo
    if next_mixer == "even":
        _project_even(xo.astype(BF16), *w_refs, *proj_refs, refs[-1], tiles_per_seq)
    elif next_mixer == "odd":
        _project_odd(xo.astype(BF16), *w_refs, *proj_refs)

    @pl.when(i == pl.num_programs(0) - 1)
    def _():
        _wait_rows(yb_hbm, ya_buf, 1 - slot, sem_a)
        _wait_rows(yb_hbm, yb_buf, 1 - slot, sem_b)


def _combine_ln(yb, pos_ext, rec, xf, lng, lnb, next_mixer, next_weights, *, seq, alpha):
    n, d = xf.shape
    tm = min(ROW_TILE, seq)
    idx_spec = lambda shift: pl.BlockSpec((1, 1, 2 * tm), lambda i: (i + shift, 0, 0),
                                          memory_space=pltpu.SMEM)
    outs = [(n, d, F32)]
    scratch = [pltpu.VMEM((2, tm * TOKEN_SLABS, LANES), F32),
               pltpu.VMEM((2, tm * TOKEN_SLABS, LANES), F32),
               pltpu.SemaphoreType.DMA((2, 1)), pltpu.SemaphoreType.DMA((2, 1))]
    if next_mixer == "even":
        outs += _even_projection_shapes(n, d, next_weights[0])
        scratch.append(pltpu.VMEM((8, LANES), F32))
    elif next_mixer == "odd":
        outs.append((n, next_weights[0].shape[1], BF16))
    return pl.pallas_call(
        functools.partial(_combine_ln_kernel, alpha=alpha, next_mixer=next_mixer,
                          tiles_per_seq=seq // tm),
        out_shape=tuple(jax.ShapeDtypeStruct((r, c), t) for r, c, t in outs),
        grid=(n // tm,),
        in_specs=[idx_spec(0), idx_spec(1),
                  pl.BlockSpec(memory_space=pl.ANY),
                  pl.BlockSpec((tm, LANES), lambda i: (i, 0)),
                  pl.BlockSpec((tm, d), lambda i: (i, 0)),
                  pl.BlockSpec(lng.shape, lambda i: (0, 0)),
                  pl.BlockSpec(lnb.shape, lambda i: (0, 0))]
                 + [pl.BlockSpec(w.shape, lambda i: (0, 0)) for w in next_weights],
        out_specs=tuple(pl.BlockSpec((tm, c), lambda i: (i, 0)) for _, c, _ in outs),
        scratch_shapes=scratch,
        compiler_params=_params(1),
        name="moe_combine_ln",
    )(pos_ext, pos_ext, yb, rec, xf, lng, lnb, *next_weights)


def kernel(x, ab_w_in, ab_b_forget, ab_conv_w, ab_w_out, c_w_in, c_lam_q1, c_lam_k1, c_lam_q2,
           c_lam_k2, c_subln_g, c_w_out, ln_mix_g, ln_mix_b, ln_ffn_g, ln_ffn_b, moe_w_group,
           moe_b_group, moe_w_expert, moe_b_expert, moe_w_gate, moe_w_up, moe_w_down):
    batch, seq, d = x.shape
    depth = ln_mix_g.shape[0]
    n_tok = batch * seq
    alpha = (2.0 * depth) ** 0.25
    q_scale = HEAD_DIM ** -0.5 * LOG2E
    conv_ch = ab_conv_w.shape[1]
    fox_heads = ab_b_forget.shape[1]
    n_diff_heads = d // (2 * HEAD_DIM)
    slopes = jnp.asarray(2.0 ** (-8.0 * np.arange(1, n_diff_heads + 1) / n_diff_heads), F32)
    xf = x.reshape(n_tok, d)

    def pad_lanes(v):
        return jnp.pad(v, (0, LANES - v.shape[0]))

    def projection_weights(layer):
        i = layer // 2
        if layer % 2 == 0:
            n_main = ab_w_in.shape[2] - fox_heads
            q_lo = 3 * conv_ch
            col_scale = np.ones((n_main,), np.float32)
            col_scale[q_lo:q_lo + fox_heads * HEAD_DIM] = q_scale
            w_main = (ab_w_in[i, :, :n_main] * col_scale).astype(BF16)
            wf = jnp.pad(ab_w_in[i, :, n_main:], ((0, 0), (0, LANES - fox_heads))).astype(BF16)
            return "even", (w_main, wf, pad_lanes(ab_b_forget[i])[None, :])
        col_scale = np.ones((c_w_in.shape[2],), np.float32)
        col_scale[0:d] = q_scale
        return "odd", ((c_w_in[i] * col_scale).astype(BF16),)

    projected = _inproj_even(xf, *projection_weights(0)[1], seq=seq)
    for layer in range(depth):
        i = layer // 2
        w_router = jnp.pad(jnp.concatenate([moe_w_group[layer], moe_w_expert[layer]], axis=1),
                           ((0, 0), (0, LANES - N_GROUPS - N_EXPERTS)))
        w_router_hi = w_router.astype(BF16)
        w_router = jnp.concatenate(
            [w_router_hi, (w_router - w_router_hi.astype(F32)).astype(BF16)], axis=0)
        b_router = pad_lanes(jnp.concatenate([moe_b_group[layer], moe_b_expert[layer]]))[None, :]
        lng, lnb = ln_mix_g[layer][None, :], ln_mix_b[layer][None, :]
        if layer % 2 == 0:
            gates, qkv, csplit = projected
            attn = _fox_attention(qkv, csplit, batch=batch, seq=seq)
            convw = jnp.pad(jnp.transpose(ab_conv_w[i]), ((0, 8 - ab_conv_w.shape[2]), (0, 0)))
            xf, rec = _outproj((gates, attn), xf, ab_w_out[i].astype(BF16), convw, lng, lnb,
                               w_router, b_router, seq=seq, alpha=alpha)
        else:
            (h,) = projected
            lam_init = 0.8 - 0.6 * math.exp(-0.3 * layer)
            lam_rows = jnp.pad(jnp.stack([pad_lanes(c_lam_q1[i]), pad_lanes(c_lam_k1[i]),
                                          pad_lanes(c_lam_q2[i]), pad_lanes(c_lam_k2[i])]),
                               ((0, 4), (0, 0)))
            attn = _diff_attention(h, slopes, lam_rows, c_subln_g[i][None, :],
                                   batch=batch, seq=seq, lam_init=lam_init)
            xf, rec = _outproj((attn,), xf, c_w_out[i].astype(BF16), None, lng, lnb,
                               w_router, b_router, seq=seq, alpha=alpha)

        src_ext, pos_ext, blk_e = _dispatch_plan(rec, n_tok, MOE_ROWS)
        yb = _expert_mlp(xf, moe_w_gate, moe_w_up, moe_w_down, layer, src_ext, blk_e, MOE_ROWS)
        next_mixer, next_weights = projection_weights(layer + 1) if layer + 1 < depth else (None, ())
        xf, *projected = _combine_ln(yb, pos_ext, rec, xf, ln_ffn_g[layer][None, :],
                                     ln_ffn_b[layer][None, :], next_mixer, next_weights,
                                     seq=seq, alpha=alpha)
    return xf.reshape(batch, seq, d)
```

```python
import functools
import math

import numpy as np
import jax
import jax.numpy as jnp
from jax import lax
from jax.experimental import pallas as pl
from jax.experimental.pallas import tpu as pltpu

F32 = jnp.float32
BF16 = jnp.bfloat16

LANES = 128
HEAD_DIM = 64
N_GROUPS = 4
EXPERTS_PER_GROUP = 8
N_EXPERTS = N_GROUPS * EXPERTS_PER_GROUP
CHUNK = 64
LN_EPS = 1e-5
RMS_EPS = 1e-5
LOG2E = math.log2(math.e)
M_INIT = -1e30
MASKED = -3e38
VMEM_LIMIT = 48 * 1024 * 1024

ROW_TILE = 256
PROJ_TILE = 512
ATTN_TILE = 256
SOFTMAX_ROWS = 32
MOE_ROWS = 256
GATHER_BUFFERS = 3
ROW_SEMS = 4
TOKEN_SLABS = 8
GATHER_QUEUE = 1
COL_CHUNK = 512

_NT = (((1,), (1,)), ((), ()))


def _params(n_grid):
    return pltpu.CompilerParams(dimension_semantics=("arbitrary",) * n_grid,
                                vmem_limit_bytes=VMEM_LIMIT)


def _layer_norm(z, g, b):
    mu = jnp.mean(z, axis=-1, keepdims=True)
    zc = z - mu
    var = jnp.mean(zc * zc, axis=-1, keepdims=True)
    return zc * lax.rsqrt(var + LN_EPS) * g + b


def _split3(v):
    a0 = v.astype(BF16)
    r1 = v - a0.astype(F32)
    a1 = r1.astype(BF16)
    a2 = (r1 - a1.astype(F32)).astype(BF16)
    return a0, a1, a2


def _project_even(x, w_ref, wf_ref, bf_ref, gates_ref, qkv_ref, csplit_ref, carry_ref,
                  tiles_per_seq):
    i = pl.program_id(0)
    n_gate = gates_ref.shape[1]
    for j in range(n_gate // COL_CHUNK):
        sl = slice(j * COL_CHUNK, (j + 1) * COL_CHUNK)
        gates_ref[:, sl] = jnp.dot(x, w_ref[:, sl], preferred_element_type=F32)
    for j in range(qkv_ref.shape[1] // COL_CHUNK):
        sl = slice(j * COL_CHUNK, (j + 1) * COL_CHUNK)
        wsl = slice(n_gate + j * COL_CHUNK, n_gate + (j + 1) * COL_CHUNK)
        qkv_ref[:, sl] = jnp.dot(x, w_ref[:, wsl], preferred_element_type=F32).astype(BF16)

    z = jnp.dot(x, wf_ref[...], preferred_element_type=F32) + bf_ref[...]
    lf = jnp.minimum(z, 0.0) - jnp.log1p(jnp.exp(-jnp.abs(z)))
    tm = lf.shape[0]
    r = lax.broadcasted_iota(jnp.int32, (tm, tm), 0)
    c = lax.broadcasted_iota(jnp.int32, (tm, tm), 1)
    tri = jnp.where(c <= r, 1.0, 0.0).astype(BF16)
    a0, a1, a2 = _split3(lf)
    cs = (jnp.dot(tri, a0, preferred_element_type=F32)
          + jnp.dot(tri, a1, preferred_element_type=F32)
          + jnp.dot(tri, a2, preferred_element_type=F32))

    @pl.when(i % tiles_per_seq == 0)
    def _():
        carry_ref[...] = jnp.zeros_like(carry_ref)

    cs = cs + carry_ref[0:1, :]
    carry_ref[...] = jnp.broadcast_to(cs[tm - 1:tm, :], carry_ref.shape)
    c0, c1, c2 = _split3(cs * LOG2E)
    lane = lax.broadcasted_iota(jnp.int32, (tm, LANES), 1)
    csplit_ref[:, 0:LANES] = c0
    csplit_ref[:, LANES:2 * LANES] = c1
    csplit_ref[:, 2 * LANES:3 * LANES] = c2
    csplit_ref[:, 3 * LANES:4 * LANES] = jnp.where(lane == 0, 1.0, 0.0).astype(BF16)


def _project_odd(x, w_ref, h_ref):
    for j in range(h_ref.shape[1] // COL_CHUNK):
        sl = slice(j * COL_CHUNK, (j + 1) * COL_CHUNK)
        h_ref[:, sl] = jnp.dot(x, w_ref[:, sl], preferred_element_type=F32).astype(BF16)


def _even_projection_shapes(n, d, w_main):
    n_gate = 3 * (d // 2)
    return [(n, n_gate, F32), (n, w_main.shape[1] - n_gate, BF16), (n, 4 * LANES, BF16)]


def _inproj_even_kernel(x_ref, w_ref, wf_ref, bf_ref, gates_ref, qkv_ref, csplit_ref, carry_ref,
                        *, tiles_per_seq):
    _project_even(x_ref[...].astype(BF16), w_ref, wf_ref, bf_ref, gates_ref, qkv_ref, csplit_ref,
                  carry_ref, tiles_per_seq)


def _inproj_even(xf, w_main, wf, bf, *, seq):
    n, d = xf.shape
    tm = min(PROJ_TILE, seq)
    outs = _even_projection_shapes(n, d, w_main)
    return pl.pallas_call(
        functools.partial(_inproj_even_kernel, tiles_per_seq=seq // tm),
        out_shape=tuple(jax.ShapeDtypeStruct((r, c), t) for r, c, t in outs),
        grid=(n // tm,),
        in_specs=[pl.BlockSpec((tm, d), lambda i: (i, 0)),
                  pl.BlockSpec(w_main.shape, lambda i: (0, 0)),
                  pl.BlockSpec(wf.shape, lambda i: (0, 0)),
                  pl.BlockSpec(bf.shape, lambda i: (0, 0))],
        out_specs=tuple(pl.BlockSpec((tm, c), lambda i: (i, 0)) for _, c, _ in outs),
        scratch_shapes=[pltpu.VMEM((8, LANES), F32)],
        compiler_params=_params(1),
        name="inproj_even",
    )(xf, w_main, wf, bf)


def _flash_scratch(tq, seq):
    r2 = 2 * tq
    return ([pltpu.VMEM((seq, 2 * LANES), BF16)]
            + [pltpu.VMEM((r2, tq), F32) for _ in range(2)]
            + [pltpu.VMEM((r2, tq), BF16) for _ in range(2)]
            + [pltpu.VMEM((r2, LANES), F32) for _ in range(2)]
            + [pltpu.VMEM((r2, LANES), F32) for _ in range(3)])


def _stack_queries(q, feat_a, feat_b):
    lane = lax.broadcasted_iota(jnp.int32, q.shape, 1)
    zero = jnp.zeros_like(q)
    qa = jnp.concatenate([jnp.where(lane < HEAD_DIM, q, zero), feat_a], axis=1)
    qb = jnp.concatenate([jnp.where(lane >= HEAD_DIM, q, zero), feat_b], axis=1)
    return jnp.concatenate([qa, qb], axis=0)


def _flash_pipeline(qaug, kaug_ref, v_ref, scratch, qi, tq, diag_bias_ref):
    s_x, s_y, p_x, p_y, al_x, al_y, m_scr, l_scr, acc_scr = scratch
    r2 = 2 * tq
    n_rep = tq // LANES

    def scores(j, s_ref):
        off = pl.multiple_of(j * tq, tq)
        s_ref[...] = lax.dot_general(qaug, kaug_ref[pl.ds(off, tq), :], _NT,
                                     preferred_element_type=F32)

    def weighted_values(j, p_ref, al_ref):
        off = pl.multiple_of(j * tq, tq)
        acc_scr[...] = al_ref[...] * acc_scr[...] + jnp.dot(
            p_ref[...], v_ref[pl.ds(off, tq), :], preferred_element_type=F32)

    def softmax(s_ref, p_ref, al_ref, bias_fn):
        for c in range(r2 // SOFTMAX_ROWS):
            rows = slice(c * SOFTMAX_ROWS, (c + 1) * SOFTMAX_ROWS)
            x = s_ref[rows, :]
            if bias_fn is not None:
                x = x + bias_fn(c)
            m_old = m_scr[rows, :]
            m_new = jnp.maximum(m_old, jnp.max(x, axis=1, keepdims=True))
            alpha = jnp.exp2(m_old - m_new)
            p = jnp.exp2(x - jnp.concatenate([m_new] * n_rep, axis=1))
            p_sum = p[:, 0:LANES]
            for t in range(1, n_rep):
                p_sum = p_sum + p[:, t * LANES:(t + 1) * LANES]
            m_scr[rows, :] = m_new
            l_scr[rows, :] = alpha * l_scr[rows, :] + p_sum
            al_ref[rows, :] = alpha
            p_ref[rows, :] = p.astype(BF16)

    def diag_bias(c):
        r0 = (c * SOFTMAX_ROWS) % tq
        return diag_bias_ref[r0:r0 + SOFTMAX_ROWS, :]

    scores(0, s_x)
    m_scr[...] = jnp.full(m_scr.shape, M_INIT, F32)
    l_scr[...] = jnp.zeros(l_scr.shape, F32)
    acc_scr[...] = jnp.zeros(acc_scr.shape, F32)
    p_y[...] = jnp.zeros(p_y.shape, BF16)
    al_y[...] = jnp.ones(al_y.shape, F32)

    def pair(i, carry):
        e = 2 * i
        weighted_values(jnp.maximum(e - 1, 0), p_y, al_y)
        scores(e + 1, s_y)
        softmax(s_x, p_x, al_x, None)
        weighted_values(e, p_x, al_x)
        scores(e + 2, s_x)
        softmax(s_y, p_y, al_y, None)
        return carry

    lax.fori_loop(0, qi // 2, pair, 0)

    e = 2 * (qi // 2)
    gate = jnp.where(qi % 2 == 1, 0.0, MASKED).astype(F32)
    weighted_values(jnp.maximum(e - 1, 0), p_y, al_y)
    scores(qi, s_y)
    softmax(s_x, p_x, al_x, lambda c: gate)
    weighted_values(e, p_x, al_x)
    softmax(s_y, p_y, al_y, diag_bias)
    weighted_values(qi, p_y, al_y)
    return jnp.sum(l_scr[...], axis=1, keepdims=True), acc_scr[...]


def _fox_kernel(q_ref, k_ref, v_ref, cs_ref, eq_ref, ek_ref, o_ref, diag_ref, kaug_ref, *scratch,
                tq):
    kaug_ref[:, 0:LANES] = k_ref[...]
    kaug_ref[:, LANES:] = jnp.dot(cs_ref[...], ek_ref[0], preferred_element_type=F32).astype(BF16)
    row = lax.broadcasted_iota(jnp.int32, (tq, tq), 0)
    col = lax.broadcasted_iota(jnp.int32, (tq, tq), 1)
    diag_ref[...] = jnp.where(col <= row, 0.0, MASKED)

    def query_tile(qi, carry):
        rows = pl.ds(pl.multiple_of(qi * tq, tq), tq)
        cs_q = cs_ref[rows, :]
        feats = [jnp.dot(cs_q, eq_ref[0, sub], preferred_element_type=F32).astype(BF16)
                 for sub in range(2)]
        qaug = _stack_queries(q_ref[rows, :], feats[0], feats[1])
        l, acc = _flash_pipeline(qaug, kaug_ref, v_ref, scratch, qi, tq, diag_ref)
        out = acc / l
        lane = lax.broadcasted_iota(jnp.int32, (tq, LANES), 1)
        o_ref[rows, :] = jnp.where(lane < HEAD_DIM, out[0:tq], out[tq:2 * tq]).astype(o_ref.dtype)
        return carry

    lax.fori_loop(0, q_ref.shape[0] // tq, query_tile, 0)


def _fox_feature_maps(n_pairs):
    one_row = 3 * LANES
    ek = np.zeros((n_pairs, 4 * LANES, LANES), np.float32)
    eq = np.zeros((n_pairs, 2, 4 * LANES, LANES), np.float32)
    for p in range(n_pairs):
        for sub in range(2):
            head, base = 2 * p + sub, 6 * sub
            for t in range(3):
                ek[p, one_row, base + t] = 1.0
                ek[p, t * LANES + head, base + 3 + t] = -1.0
                eq[p, sub, t * LANES + head, base + t] = 1.0
                eq[p, sub, one_row, base + 3 + t] = 1.0
    return jnp.asarray(eq, BF16), jnp.asarray(ek, BF16)


def _fox_attention(qkv, csplit, *, batch, seq):
    n, width3 = qkv.shape
    n_pairs = width3 // 3 // LANES
    tq = min(ATTN_TILE, seq)
    eq, ek = _fox_feature_maps(n_pairs)
    return pl.pallas_call(
        functools.partial(_fox_kernel, tq=tq),
        out_shape=jax.ShapeDtypeStruct((n, n_pairs * LANES), BF16),
        grid=(batch, n_pairs),
        in_specs=[pl.BlockSpec((seq, LANES), lambda b, h: (b, h)),
                  pl.BlockSpec((seq, LANES), lambda b, h: (b, n_pairs + h)),
                  pl.BlockSpec((seq, LANES), lambda b, h: (b, 2 * n_pairs + h)),
                  pl.BlockSpec((seq, csplit.shape[1]), lambda b, h: (b, 0)),
                  pl.BlockSpec((1,) + eq.shape[1:], lambda b, h: (h, 0, 0, 0)),
                  pl.BlockSpec((1,) + ek.shape[1:], lambda b, h: (h, 0, 0))],
        out_specs=pl.BlockSpec((seq, LANES), lambda b, h: (b, h)),
        scratch_shapes=[pltpu.VMEM((tq, tq), F32)] + _flash_scratch(tq, seq),
        compiler_params=_params(2),
        name="fox_attention",
    )(qkv, qkv, qkv, csplit, eq, ek)


def _diff_kernel(slopes_ref, q_ref, k_ref, v_ref, lam_ref, g_ref, o_ref, diag_ref, kaug_ref,
                 *scratch, tq, lam_init):
    h = pl.program_id(1)
    seq = k_ref.shape[0]
    slope2 = slopes_ref[h] * LOG2E

    def bias_lanes(n_rows, first, base, other, other_value):
        lane = lax.broadcasted_iota(jnp.int32, (n_rows, LANES), 1)
        pos = (lax.broadcasted_iota(jnp.int32, (n_rows, LANES), 0) + first).astype(F32) * slope2
        terms = [t.astype(F32) for t in _split3(pos)]
        feat = jnp.where((lane >= other) & (lane < other + 3), other_value, 0.0)
        for t, term in enumerate(terms):
            feat = jnp.where(lane == base + t, term, feat)
        return feat.astype(BF16)

    kaug_ref[:, 0:LANES] = k_ref[...]
    kaug_ref[:, LANES:] = bias_lanes(seq, 0, 0, 3, -1.0)
    row = lax.broadcasted_iota(jnp.int32, (tq, tq), 0)
    col = lax.broadcasted_iota(jnp.int32, (tq, tq), 1)
    ahead = jnp.maximum(col - row, 0).astype(F32)
    diag_ref[...] = jnp.where((col // CHUNK) <= (row // CHUNK), -2.0 * slope2 * ahead, MASKED)
    t1 = jnp.sum(lam_ref[0:1, :] * lam_ref[1:2, :], axis=-1, keepdims=True)
    t2 = jnp.sum(lam_ref[2:3, :] * lam_ref[3:4, :], axis=-1, keepdims=True)
    lam = jnp.exp(t1) - jnp.exp(t2) + lam_init

    def query_tile(qi, carry):
        rows = pl.ds(pl.multiple_of(qi * tq, tq), tq)
        qfeat = bias_lanes(tq, qi * tq, 3, 0, 1.0)
        qaug = _stack_queries(q_ref[rows, :], qfeat, qfeat)
        l, acc = _flash_pipeline(qaug, kaug_ref, v_ref, scratch, qi, tq, diag_ref)
        out = acc / l
        o = out[0:tq] - lam * out[tq:2 * tq]
        o = o * lax.rsqrt(jnp.mean(o * o, axis=-1, keepdims=True) + RMS_EPS)
        o_ref[rows, :] = (o * g_ref[...] * (1.0 - lam_init)).astype(o_ref.dtype)
        return carry

    lax.fori_loop(0, seq // tq, query_tile, 0)


def _diff_attention(h, slopes, lam_rows, subln_g, *, batch, seq, lam_init):
    n, width3 = h.shape
    n_heads = width3 // 3 // LANES
    tq = min(ATTN_TILE, seq)
    grid_spec = pltpu.PrefetchScalarGridSpec(
        num_scalar_prefetch=1,
        grid=(batch, n_heads),
        in_specs=[pl.BlockSpec((seq, LANES), lambda b, hd, s: (b, hd)),
                  pl.BlockSpec((seq, LANES), lambda b, hd, s: (b, n_heads + hd)),
                  pl.BlockSpec((seq, LANES), lambda b, hd, s: (b, 2 * n_heads + hd)),
                  pl.BlockSpec(lam_rows.shape, lambda b, hd, s: (0, 0)),
                  pl.BlockSpec(subln_g.shape, lambda b, hd, s: (0, 0))],
        out_specs=pl.BlockSpec((seq, LANES), lambda b, hd, s: (b, hd)),
        scratch_shapes=[pltpu.VMEM((tq, tq), F32)] + _flash_scratch(tq, seq),
    )
    return pl.pallas_call(
        functools.partial(_diff_kernel, tq=tq, lam_init=lam_init),
        out_shape=jax.ShapeDtypeStruct((n, n_heads * LANES), BF16),
        grid_spec=grid_spec,
        compiler_params=_params(2),
        name="diff_attention",
    )(slopes, h, h, h, lam_rows, subln_g)


def _route(lg):
    lane = lax.broadcasted_iota(jnp.int32, lg.shape, 1).astype(F32)
    low = jnp.float32(-3e38)
    is_g = lane < N_GROUPS
    gmax = jnp.max(jnp.where(is_g, lg, low), axis=-1, keepdims=True)
    gsel = jnp.min(jnp.where(is_g & (lg == gmax), lane, float(LANES)), axis=-1, keepdims=True)
    gsum = jnp.sum(jnp.where(is_g, jnp.exp(lg - gmax), 0.0), axis=-1, keepdims=True)
    gw = 1.0 / gsum
    lo = N_GROUPS + EXPERTS_PER_GROUP * gsel
    in_grp = (lane >= lo) & (lane < lo + EXPERTS_PER_GROUP)
    el = jnp.where(in_grp, lg, low)
    v1 = jnp.max(el, axis=-1, keepdims=True)
    i1 = jnp.min(jnp.where(in_grp & (lg == v1), lane, float(LANES)), axis=-1, keepdims=True)
    rest = in_grp & (lane != i1)
    v2 = jnp.max(jnp.where(rest, lg, low), axis=-1, keepdims=True)
    i2 = jnp.min(jnp.where(rest & (lg == v2), lane, float(LANES)), axis=-1, keepdims=True)
    t = jnp.exp(v2 - v1)
    w1 = gw / (1.0 + t)
    w2 = gw * (t / (1.0 + t))
    rec = jnp.where(lane == 0.0, i1 - N_GROUPS,
                    jnp.where(lane == 1.0, i2 - N_GROUPS,
                              jnp.where(lane == 2.0, w1, jnp.where(lane == 3.0, w2, 0.0))))
    return rec


def _mix_ln_route(mix, x_ref, lng_ref, lnb_ref, wr_ref, br_ref, xo_ref, xs_ref, rec_ref, alpha):
    xo = _layer_norm(alpha * x_ref[...] + mix, lng_ref[...], lnb_ref[...])
    xo_ref[...] = xo
    tm = xo.shape[0]
    for j in range(TOKEN_SLABS):
        xs_ref[pl.ds(j, tm, stride=TOKEN_SLABS), :] = xo[:, j * LANES:(j + 1) * LANES]
    d = xo.shape[1]
    x_hi = xo.astype(BF16)
    x_lo = (xo - x_hi.astype(F32)).astype(BF16)
    lg = (jnp.dot(x_hi, wr_ref[0:d, :], preferred_element_type=F32)
          + jnp.dot(x_lo, wr_ref[0:d, :], preferred_element_type=F32)
          + jnp.dot(x_hi, wr_ref[d:2 * d, :], preferred_element_type=F32)) + br_ref[...]
    rec_ref[...] = _route(lg)


def _outproj_even_kernel(gates_ref, attn_ref, x_ref, wout_ref, convw_ref, lng_ref, lnb_ref,
                         wr_ref, br_ref, xo_ref, xs_ref, rec_ref, tail_ref, *, tiles_per_seq,
                         alpha):
    i = pl.program_id(0)
    tm = x_ref.shape[0]
    cw = gates_ref.shape[1] // 3
    gb = gates_ref[:, 0:cw]
    u = gates_ref[:, cw:2 * cw] * gates_ref[:, 2 * cw:3 * cw]

    @pl.when(i % tiles_per_seq == 0)
    def _():
        tail_ref[...] = jnp.zeros_like(tail_ref)

    tail = tail_ref[...]
    row = lax.broadcasted_iota(jnp.int32, (tm, 1), 0)
    u1 = jnp.where(row == 0, tail[7:8, :], pltpu.roll(u, 1, axis=0))
    u2 = jnp.where(row == 0, tail[6:7, :],
                   jnp.where(row == 1, tail[7:8, :], pltpu.roll(u, 2, axis=0)))
    tail_ref[...] = u[tm - 8:tm, :]
    y = u2 * convw_ref[0:1, :] + u1 * convw_ref[1:2, :] + u * convw_ref[2:3, :]
    a = (gb * y).astype(BF16)
    mix = (jnp.dot(a, wout_ref[0:cw, :], preferred_element_type=F32)
           + jnp.dot(attn_ref[...], wout_ref[cw:, :], preferred_element_type=F32))
    _mix_ln_route(mix, x_ref, lng_ref, lnb_ref, wr_ref, br_ref, xo_ref, xs_ref, rec_ref, alpha)


def _outproj_odd_kernel(attn_ref, x_ref, wout_ref, lng_ref, lnb_ref, wr_ref, br_ref,
                        xo_ref, xs_ref, rec_ref, *, alpha):
    mix = jnp.dot(attn_ref[...], wout_ref[...], preferred_element_type=F32)
    _mix_ln_route(mix, x_ref, lng_ref, lnb_ref, wr_ref, br_ref, xo_ref, xs_ref, rec_ref, alpha)


def _outproj(acts, xf, wout, convw, lng, lnb, wr, br, *, seq, alpha):
    n, d = xf.shape
    tm = min(ROW_TILE, seq)
    row_spec = lambda width: pl.BlockSpec((tm, width), lambda i: (i, 0))
    full = lambda a: pl.BlockSpec(a.shape, lambda i: (0, 0))
    even = convw is not None
    if even:
        body = functools.partial(_outproj_even_kernel, tiles_per_seq=seq // tm, alpha=alpha)
        operands = (*acts, xf, wout, convw, lng, lnb, wr, br)
        scratch = [pltpu.VMEM((8, acts[0].shape[1] // 3), F32)]
    else:
        body = functools.partial(_outproj_odd_kernel, alpha=alpha)
        operands = (*acts, xf, wout, lng, lnb, wr, br)
        scratch = []
    in_specs = [row_spec(a.shape[1]) for a in acts] + [row_spec(d)]
    in_specs += [full(a) for a in operands[len(acts) + 1:]]
    return pl.pallas_call(
        body,
        out_shape=(jax.ShapeDtypeStruct((n, d), F32),
                   jax.ShapeDtypeStruct((n * TOKEN_SLABS, LANES), F32),
                   jax.ShapeDtypeStruct((n, LANES), F32)),
        grid=(n // tm,),
        in_specs=in_specs,
        out_specs=(row_spec(d), pl.BlockSpec((tm * TOKEN_SLABS, LANES), lambda i: (i, 0)),
                   row_spec(LANES)),
        scratch_shapes=scratch,
        compiler_params=_params(1),
        name="outproj_even" if even else "outproj_odd",
    )(*operands)


def _plan_kernel(rec_ref, start_ref, pos_ref, run_ref):
    i = pl.program_id(0)

    @pl.when(i == 0)
    def _():
        run_ref[...] = jnp.zeros_like(run_ref)

    rec = rec_ref[...]
    tm = rec.shape[0]
    lane = lax.broadcasted_iota(jnp.int32, rec.shape, 1).astype(F32)
    oh1 = lane == rec[:, 0:1]
    oh2 = lane == rec[:, 1:2]
    oh = jnp.where(oh1 | oh2, 1.0, 0.0)
    r = lax.broadcasted_iota(jnp.int32, (tm, tm), 0)
    c = lax.broadcasted_iota(jnp.int32, (tm, tm), 1)
    earlier = jnp.where(c < r, 1.0, 0.0).astype(BF16)
    base = (jnp.dot(earlier, oh.astype(BF16), preferred_element_type=F32)
            + run_ref[0:1, :] + start_ref[...])
    p1 = jnp.sum(jnp.where(oh1, base, 0.0), axis=1, keepdims=True)
    p2 = jnp.sum(jnp.where(oh2, base, 0.0), axis=1, keepdims=True)
    run_ref[...] = run_ref[...] + jnp.sum(oh, axis=0, keepdims=True)
    pos_ref[...] = jnp.where(lane == 0.0, p1, jnp.where(lane == 1.0, p2, 0.0)).astype(jnp.int32)


def _plan_rows(rec, seg_start):
    n = rec.shape[0]
    tm = ROW_TILE
    return pl.pallas_call(
        _plan_kernel,
        out_shape=jax.ShapeDtypeStruct((n, LANES), jnp.int32),
        grid=(n // tm,),
        in_specs=[pl.BlockSpec((tm, LANES), lambda i: (i, 0)),
                  pl.BlockSpec(seg_start.shape, lambda i: (0, 0))],
        out_specs=pl.BlockSpec((tm, LANES), lambda i: (i, 0)),
        scratch_shapes=[pltpu.VMEM((8, LANES), F32)],
        compiler_params=_params(1),
        name="moe_plan",
    )(rec, seg_start)


def _dispatch_plan(rec, n_tok, rows):
    eid = rec[:, 0:2].astype(jnp.int32).reshape(-1)
    counts = jnp.sum((eid[:, None] == jnp.arange(N_EXPERTS, dtype=jnp.int32)[None, :])
                     .astype(jnp.int32), axis=0)
    padded = (counts + rows - 1) // rows * rows
    pad_end = jnp.cumsum(padded)
    seg_start = jnp.pad((pad_end - padded).astype(F32), (0, LANES - N_EXPERTS))[None, :]
    pos = _plan_rows(rec, seg_start)[:, 0:2].reshape(-1)
    cap = 2 * n_tok + N_EXPERTS * rows
    n_blocks = cap // rows
    n_pairs = 2 * n_tok
    row_pair = jnp.full((cap,), -1, jnp.int32).at[pos].set(jnp.arange(n_pairs, dtype=jnp.int32))
    src = jnp.where(row_pair >= 0, (row_pair >> 1) * TOKEN_SLABS, 0)
    look_ahead = GATHER_BUFFERS - 1
    src_ext = jnp.concatenate([src, jnp.zeros((look_ahead * rows,), jnp.int32)])
    src_ext = src_ext.reshape(n_blocks + look_ahead, 1, rows)
    tm = ROW_TILE
    n_tiles = n_tok // tm
    pos_ext = jnp.transpose((pos * TOKEN_SLABS).reshape(n_tiles, tm, 2), (0, 2, 1))
    pos_ext = pos_ext.reshape(n_tiles, 2 * tm)
    pos_ext = jnp.concatenate([pos_ext, jnp.zeros((1, 2 * tm), jnp.int32)])
    pos_ext = pos_ext.reshape(n_tiles + 1, 1, 2 * tm)
    blk_start = jnp.arange(n_blocks, dtype=jnp.int32) * rows
    blk_e = jnp.minimum(jnp.sum((pad_end[None, :] <= blk_start[:, None]).astype(jnp.int32), axis=1),
                        N_EXPERTS - 1)
    return src_ext, pos_ext, blk_e


def _start_rows(src_hbm, idx_ref, idx_base, dst_ref, slot, sem, n, priority=None):
    n_sems = sem.shape[1]
    for r in range(n):
        first = pl.multiple_of(idx_ref[0, 0, idx_base + r], TOKEN_SLABS)
        pltpu.make_async_copy(src_hbm.at[pl.ds(first, TOKEN_SLABS)],
                              dst_ref.at[slot, pl.ds(r * TOKEN_SLABS, TOKEN_SLABS)],
                              sem.at[slot, r % n_sems]
                              ).start(priority=r % 2 if priority is None else priority)


def _wait_rows(src_hbm, dst_ref, slot, sem):
    n_sems = sem.shape[1]
    share = dst_ref.shape[1] // n_sems
    for k in range(n_sems):
        pltpu.make_async_copy(src_hbm.at[pl.ds(0, share)], dst_ref.at[slot, pl.ds(0, share)],
                              sem.at[slot, k]).wait()


def _expert_mlp_kernel(blk_e_ref, src0_ref, src1_ref, src2_ref, x_hbm, wg_ref, wu_ref, wd_ref,
                       out_ref, xbuf, wgu_s, wd_s, sem):
    i = pl.program_id(0)
    rows = xbuf.shape[1] // TOKEN_SLABS
    dh = wg_ref.shape[3]
    slot = i % GATHER_BUFFERS
    ahead1 = (i + 1) % GATHER_BUFFERS
    ahead2 = (i + 2) % GATHER_BUFFERS

    @pl.when(i == 0)
    def _():
        _start_rows(x_hbm, src0_ref, 0, xbuf, 0, sem, rows, GATHER_QUEUE)
        _start_rows(x_hbm, src1_ref, 0, xbuf, 1, sem, rows, GATHER_QUEUE)

    _wait_rows(x_hbm, xbuf, slot, sem)
    changed = jnp.logical_or(i == 0, blk_e_ref[i] != blk_e_ref[jnp.maximum(i - 1, 0)])

    @pl.when(changed)
    def _():
        wgu_s[:, 0:dh] = wg_ref[0, 0].astype(BF16)
        wgu_s[:, dh:2 * dh] = wu_ref[0, 0].astype(BF16)
        wd_s[...] = wd_ref[0, 0].astype(BF16)

    x = jnp.concatenate([xbuf[slot, pl.ds(j, rows, stride=TOKEN_SLABS), :].astype(BF16)
                         for j in range(TOKEN_SLABS)], axis=1)
    _start_rows(x_hbm, src2_ref, 0, xbuf, ahead2, sem, rows, GATHER_QUEUE)
    gu = jnp.dot(x, wgu_s[...], preferred_element_type=F32)
    g = gu[:, 0:dh]
    hid = (g / (1.0 + jnp.exp(-g))) * gu[:, dh:2 * dh]
    y = jnp.dot(hid.astype(BF16), wd_s[...], preferred_element_type=F32)
    for j in range(TOKEN_SLABS):
        out_ref[pl.ds(j, rows, stride=TOKEN_SLABS), :] = y[:, j * LANES:(j + 1) * LANES]

    @pl.when(i == pl.num_programs(0) - 1)
    def _():
        _wait_rows(x_hbm, xbuf, ahead1, sem)
        _wait_rows(x_hbm, xbuf, ahead2, sem)


def _expert_mlp(x_slabs, w_gate, w_up, w_down, layer, src_ext, blk_e, rows):
    d = w_gate.shape[2]
    dh = w_gate.shape[3]
    n_blocks = blk_e.shape[0]
    idx_spec = lambda shift: pl.BlockSpec((1, 1, rows), lambda i, be: (i + shift, 0, 0),
                                          memory_space=pltpu.SMEM)
    grid_spec = pltpu.PrefetchScalarGridSpec(
        num_scalar_prefetch=1,
        grid=(n_blocks,),
        in_specs=[idx_spec(0), idx_spec(1), idx_spec(2),
                  pl.BlockSpec(memory_space=pl.ANY),
                  pl.BlockSpec((1, 1, d, dh), lambda i, be: (layer, be[i], 0, 0)),
                  pl.BlockSpec((1, 1, d, dh), lambda i, be: (layer, be[i], 0, 0)),
                  pl.BlockSpec((1, 1, dh, d), lambda i, be: (layer, be[i], 0, 0))],
        out_specs=pl.BlockSpec((rows * TOKEN_SLABS, LANES), lambda i, be: (i, 0)),
        scratch_shapes=[pltpu.VMEM((GATHER_BUFFERS, rows * TOKEN_SLABS, LANES), F32),
                        pltpu.VMEM((d, 2 * dh), BF16), pltpu.VMEM((dh, d), BF16),
                        pltpu.SemaphoreType.DMA((GATHER_BUFFERS, ROW_SEMS))],
    )
    return pl.pallas_call(
        _expert_mlp_kernel,
        out_shape=jax.ShapeDtypeStruct((n_blocks * rows * TOKEN_SLABS, LANES), F32),
        grid_spec=grid_spec,
        compiler_params=_params(1),
        name="moe_experts",
    )(blk_e, src_ext, src_ext, src_ext, x_slabs, w_gate, w_up, w_down)


def _combine_ln_kernel(pos0_ref, pos1_ref, yb_hbm, rec_ref, x_ref, lng_ref, lnb_ref, *refs,
                       alpha, next_mixer, tiles_per_seq):
    n_w = {"even": 3, "odd": 1, None: 0}[next_mixer]
    n_out = {"even": 3, "odd": 1, None: 0}[next_mixer]
    w_refs = refs[:n_w]
    xo_ref = refs[n_w]
    proj_refs = refs[n_w + 1:n_w + 1 + n_out]
    ya_buf, yb_buf, sem_a, sem_b = refs[n_w + 1 + n_out:n_w + 5 + n_out]
    i = pl.program_id(0)
    tm = x_ref.shape[0]
    slot = i % 2

    @pl.when(i == 0)
    def _():
        _start_rows(yb_hbm, pos0_ref, 0, ya_buf, 0, sem_a, tm)
        _start_rows(yb_hbm, pos0_ref, tm, yb_buf, 0, sem_b, tm)

    _wait_rows(yb_hbm, ya_buf, slot, sem_a)
    _wait_rows(yb_hbm, yb_buf, slot, sem_b)

    def token_rows(buf):
        return jnp.concatenate([buf[slot, pl.ds(j, tm, stride=TOKEN_SLABS), :]
                                for j in range(TOKEN_SLABS)], axis=1)

    z = alpha * x_ref[...] + (rec_ref[:, 2:3] * token_rows(ya_buf)
                              + rec_ref[:, 3:4] * token_rows(yb_buf))
    _start_rows(yb_hbm, pos1_ref, 0, ya_buf, 1 - slot, sem_a, tm)
    _start_rows(yb_hbm, pos1_ref, tm, yb_buf, 1 - slot, sem_b, tm)
    xo = _layer_norm(z, lng_ref[...], lnb_ref[...])
    xo_ref[...] = xo
    if next_mixer == "even":
        _project_even(xo.astype(BF16), *w_refs, *proj_refs, refs[-1], tiles_per_seq)
    elif next_mixer == "odd":
        _project_odd(xo.astype(BF16), *w_refs, *proj_refs)

    @pl.when(i == pl.num_programs(0) - 1)
    def _():
        _wait_rows(yb_hbm, ya_buf, 1 - slot, sem_a)
        _wait_rows(yb_hbm, yb_buf, 1 - slot, sem_b)


def _combine_ln(yb, pos_ext, rec, xf, lng, lnb, next_mixer, next_weights, *, seq, alpha):
    n, d = xf.shape
    tm = min(ROW_TILE, seq)
    idx_spec = lambda shift: pl.BlockSpec((1, 1, 2 * tm), lambda i: (i + shift, 0, 0),
                                          memory_space=pltpu.SMEM)
    outs = [(n, d, F32)]
    scratch = [pltpu.VMEM((2, tm * TOKEN_SLABS, LANES), F32),
               pltpu.VMEM((2, tm * TOKEN_SLABS, LANES), F32),
               pltpu.SemaphoreType.DMA((2, 1)), pltpu.SemaphoreType.DMA((2, 1))]
    if next_mixer == "even":
        outs += _even_projection_shapes(n, d, next_weights[0])
        scratch.append(pltpu.VMEM((8, LANES), F32))
    elif next_mixer == "odd":
        outs.append((n, next_weights[0].shape[1], BF16))
    return pl.pallas_call(
        functools.partial(_combine_ln_kernel, alpha=alpha, next_mixer=next_mixer,
                          tiles_per_seq=seq // tm),
        out_shape=tuple(jax.ShapeDtypeStruct((r, c), t) for r, c, t in outs),
        grid=(n // tm,),
        in_specs=[idx_spec(0), idx_spec(1),
                  pl.BlockSpec(memory_space=pl.ANY),
                  pl.BlockSpec((tm, LANES), lambda i: (i, 0)),
                  pl.BlockSpec((tm, d), lambda i: (i, 0)),
                  pl.BlockSpec(lng.shape, lambda i: (0, 0)),
                  pl.BlockSpec(lnb.shape, lambda i: (0, 0))]
                 + [pl.BlockSpec(w.shape, lambda i: (0, 0)) for w in next_weights],
        out_specs=tuple(pl.BlockSpec((tm, c), lambda i: (i, 0)) for _, c, _ in outs),
        scratch_shapes=scratch,
        compiler_params=_params(1),
        name="moe_combine_ln",
    )(pos_ext, pos_ext, yb, rec, xf, lng, lnb, *next_weights)


def kernel(x, ab_w_in, ab_b_forget, ab_conv_w, ab_w_out, c_w_in, c_lam_q1, c_lam_k1, c_lam_q2,
           c_lam_k2, c_subln_g, c_w_out, ln_mix_g, ln_mix_b, ln_ffn_g, ln_ffn_b, moe_w_group,
           moe_b_group, moe_w_expert, moe_b_expert, moe_w_gate, moe_w_up, moe_w_down):
    batch, seq, d = x.shape
    depth = ln_mix_g.shape[0]
    n_tok = batch * seq
    alpha = (2.0 * depth) ** 0.25
    q_scale = HEAD_DIM ** -0.5 * LOG2E
    conv_ch = ab_conv_w.shape[1]
    fox_heads = ab_b_forget.shape[1]
    n_diff_heads = d // (2 * HEAD_DIM)
    slopes = jnp.asarray(2.0 ** (-8.0 * np.arange(1, n_diff_heads + 1) / n_diff_heads), F32)
    xf = x.reshape(n_tok, d)

    def pad_lanes(v):
        return jnp.pad(v, (0, LANES - v.shape[0]))

    def projection_weights(layer):
        i = layer // 2
        if layer % 2 == 0:
            n_main = ab_w_in.shape[2] - fox_heads
            q_lo = 3 * conv_ch
            col_scale = np.ones((n_main,), np.float32)
            col_scale[q_lo:q_lo + fox_heads * HEAD_DIM] = q_scale
            w_main = (ab_w_in[i, :, :n_main] * col_scale).astype(BF16)
            wf = jnp.pad(ab_w_in[i, :, n_main:], ((0, 0), (0, LANES - fox_heads))).astype(BF16)
            return "even", (w_main, wf, pad_lanes(ab_b_forget[i])[None, :])
        col_scale = np.ones((c_w_in.shape[2],), np.float32)
        col_scale[0:d] = q_scale
        return "odd", ((c_w_in[i] * col_scale).astype(BF16),)

    projected = _inproj_even(xf, *projection_weights(0)[1], seq=seq)
    for layer in range(depth):
        i = layer // 2
        w_router = jnp.pad(jnp.concatenate([moe_w_group[layer], moe_w_expert[layer]], axis=1),
                           ((0, 0), (0, LANES - N_GROUPS - N_EXPERTS)))
        w_router_hi = w_router.astype(BF16)
        w_router = jnp.concatenate(
            [w_router_hi, (w_router - w_router_hi.astype(F32)).astype(BF16)], axis=0)
        b_router = pad_lanes(jnp.concatenate([moe_b_group[layer], moe_b_expert[layer]]))[None, :]
        lng, lnb = ln_mix_g[layer][None, :], ln_mix_b[layer][None, :]
        if layer % 2 == 0:
            gates, qkv, csplit = projected
            attn = _fox_attention(qkv, csplit, batch=batch, seq=seq)
            convw = jnp.pad(jnp.transpose(ab_conv_w[i]), ((0, 8 - ab_conv_w.shape[2]), (0, 0)))
            xf, x_slabs, rec = _outproj((gates, attn), xf, ab_w_out[i].astype(BF16), convw, lng,
                                        lnb, w_router, b_router, seq=seq, alpha=alpha)
        else:
            (h,) = projected
            lam_init = 0.8 - 0.6 * math.exp(-0.3 * layer)
            lam_rows = jnp.pad(jnp.stack([pad_lanes(c_lam_q1[i]), pad_lanes(c_lam_k1[i]),
                                          pad_lanes(c_lam_q2[i]), pad_lanes(c_lam_k2[i])]),
                               ((0, 4), (0, 0)))
            attn = _diff_attention(h, slopes, lam_rows, c_subln_g[i][None, :],
                                   batch=batch, seq=seq, lam_init=lam_init)
            xf, x_slabs, rec = _outproj((attn,), xf, c_w_out[i].astype(BF16), None, lng, lnb,
                                        w_router, b_router, seq=seq, alpha=alpha)

        src_ext, pos_ext, blk_e = _dispatch_plan(rec, n_tok, MOE_ROWS)
        yb = _expert_mlp(x_slabs, moe_w_gate, moe_w_up, moe_w_down, layer, src_ext, blk_e,
                         MOE_ROWS)
        next_mixer, next_weights = projection_weights(layer + 1) if layer + 1 < depth else (None, ())
        xf, *projected = _combine_ln(yb, pos_ext, rec, xf, ln_ffn_g[layer][None, :],
                                     ln_ffn_b[layer][None, :], next_mixer, next_weights,
                                     seq=seq, alpha=alpha)
    return xf.reshape(batch, seq, d)
```

```python
import functools
import math

import numpy as np
import jax
import jax.numpy as jnp
from jax import lax
from jax.experimental import pallas as pl
from jax.experimental.pallas import tpu as pltpu

F32 = jnp.float32
BF16 = jnp.bfloat16

LANES = 128
HEAD_DIM = 64
N_GROUPS = 4
EXPERTS_PER_GROUP = 8
N_EXPERTS = N_GROUPS * EXPERTS_PER_GROUP
CHUNK = 64
LN_EPS = 1e-5
RMS_EPS = 1e-5
LOG2E = math.log2(math.e)
M_INIT = -1e30
MASKED = -3e38
VMEM_LIMIT = 48 * 1024 * 1024

ROW_TILE = 256
PROJ_TILE = 512
ATTN_TILE = 256
SOFTMAX_ROWS = 32
MOE_ROWS = 256
GATHER_BUFFERS = 3
ROW_SEMS = 4
TOKEN_SLABS = 8
GATHER_QUEUE = 1
COL_CHUNK = 512

_NT = (((1,), (1,)), ((), ()))


def _params(n_grid):
    return pltpu.CompilerParams(dimension_semantics=("arbitrary",) * n_grid,
                                vmem_limit_bytes=VMEM_LIMIT)


def _layer_norm(z, g, b):
    mu = jnp.mean(z, axis=-1, keepdims=True)
    zc = z - mu
    var = jnp.mean(zc * zc, axis=-1, keepdims=True)
    return zc * lax.rsqrt(var + LN_EPS) * g + b


def _split3(v):
    a0 = v.astype(BF16)
    r1 = v - a0.astype(F32)
    a1 = r1.astype(BF16)
    a2 = (r1 - a1.astype(F32)).astype(BF16)
    return a0, a1, a2


def _project_even(x, w_ref, wf_ref, bf_ref, gates_ref, qkv_ref, csplit_ref, carry_ref,
                  tiles_per_seq):
    i = pl.program_id(0)
    n_gate = gates_ref.shape[1]
    for j in range(n_gate // COL_CHUNK):
        sl = slice(j * COL_CHUNK, (j + 1) * COL_CHUNK)
        gates_ref[:, sl] = jnp.dot(x, w_ref[:, sl], preferred_element_type=F32)
    for j in range(qkv_ref.shape[1] // COL_CHUNK):
        sl = slice(j * COL_CHUNK, (j + 1) * COL_CHUNK)
        wsl = slice(n_gate + j * COL_CHUNK, n_gate + (j + 1) * COL_CHUNK)
        qkv_ref[:, sl] = jnp.dot(x, w_ref[:, wsl], preferred_element_type=F32).astype(BF16)

    z = jnp.dot(x, wf_ref[...], preferred_element_type=F32) + bf_ref[...]
    lf = jnp.minimum(z, 0.0) - jnp.log1p(jnp.exp(-jnp.abs(z)))
    tm = lf.shape[0]
    r = lax.broadcasted_iota(jnp.int32, (tm, tm), 0)
    c = lax.broadcasted_iota(jnp.int32, (tm, tm), 1)
    tri = jnp.where(c <= r, 1.0, 0.0).astype(BF16)
    a0, a1, a2 = _split3(lf)
    cs = (jnp.dot(tri, a0, preferred_element_type=F32)
          + jnp.dot(tri, a1, preferred_element_type=F32)
          + jnp.dot(tri, a2, preferred_element_type=F32))

    @pl.when(i % tiles_per_seq == 0)
    def _():
        carry_ref[...] = jnp.zeros_like(carry_ref)

    cs = cs + carry_ref[0:1, :]
    carry_ref[...] = jnp.broadcast_to(cs[tm - 1:tm, :], carry_ref.shape)
    c0, c1, c2 = _split3(cs * LOG2E)
    lane = lax.broadcasted_iota(jnp.int32, (tm, LANES), 1)
    csplit_ref[:, 0:LANES] = c0
    csplit_ref[:, LANES:2 * LANES] = c1
    csplit_ref[:, 2 * LANES:3 * LANES] = c2
    csplit_ref[:, 3 * LANES:4 * LANES] = jnp.where(lane == 0, 1.0, 0.0).astype(BF16)


def _project_odd(x, w_ref, h_ref):
    for j in range(h_ref.shape[1] // COL_CHUNK):
        sl = slice(j * COL_CHUNK, (j + 1) * COL_CHUNK)
        h_ref[:, sl] = jnp.dot(x, w_ref[:, sl], preferred_element_type=F32).astype(BF16)


def _even_projection_shapes(n, d, w_main):
    n_gate = 3 * (d // 2)
    return [(n, n_gate, F32), (n, w_main.shape[1] - n_gate, BF16), (n, 4 * LANES, BF16)]


def _inproj_even_kernel(x_ref, w_ref, wf_ref, bf_ref, gates_ref, qkv_ref, csplit_ref, carry_ref,
                        *, tiles_per_seq):
    _project_even(x_ref[...].astype(BF16), w_ref, wf_ref, bf_ref, gates_ref, qkv_ref, csplit_ref,
                  carry_ref, tiles_per_seq)


def _inproj_even(xf, w_main, wf, bf, *, seq):
    n, d = xf.shape
    tm = min(PROJ_TILE, seq)
    outs = _even_projection_shapes(n, d, w_main)
    return pl.pallas_call(
        functools.partial(_inproj_even_kernel, tiles_per_seq=seq // tm),
        out_shape=tuple(jax.ShapeDtypeStruct((r, c), t) for r, c, t in outs),
        grid=(n // tm,),
        in_specs=[pl.BlockSpec((tm, d), lambda i: (i, 0)),
                  pl.BlockSpec(w_main.shape, lambda i: (0, 0)),
                  pl.BlockSpec(wf.shape, lambda i: (0, 0)),
                  pl.BlockSpec(bf.shape, lambda i: (0, 0))],
        out_specs=tuple(pl.BlockSpec((tm, c), lambda i: (i, 0)) for _, c, _ in outs),
        scratch_shapes=[pltpu.VMEM((8, LANES), F32)],
        compiler_params=_params(1),
        name="inproj_even",
    )(xf, w_main, wf, bf)


def _flash_scratch(tq, seq):
    r2 = 2 * tq
    return ([pltpu.VMEM((seq, 2 * LANES), BF16)]
            + [pltpu.VMEM((r2, tq), F32) for _ in range(2)]
            + [pltpu.VMEM((r2, tq), BF16) for _ in range(2)]
            + [pltpu.VMEM((r2, LANES), F32) for _ in range(2)]
            + [pltpu.VMEM((r2, LANES), F32) for _ in range(3)])


def _stack_queries(q, feat_a, feat_b):
    lane = lax.broadcasted_iota(jnp.int32, q.shape, 1)
    zero = jnp.zeros_like(q)
    qa = jnp.concatenate([jnp.where(lane < HEAD_DIM, q, zero), feat_a], axis=1)
    qb = jnp.concatenate([jnp.where(lane >= HEAD_DIM, q, zero), feat_b], axis=1)
    return jnp.concatenate([qa, qb], axis=0)


def _flash_pipeline(qaug, kaug_ref, v_ref, scratch, qi, tq, diag_bias_ref):
    s_x, s_y, p_x, p_y, al_x, al_y, m_scr, l_scr, acc_scr = scratch
    r2 = 2 * tq
    n_rep = tq // LANES

    def scores(j, s_ref):
        off = pl.multiple_of(j * tq, tq)
        s_ref[...] = lax.dot_general(qaug, kaug_ref[pl.ds(off, tq), :], _NT,
                                     preferred_element_type=F32)

    def weighted_values(j, p_ref, al_ref):
        off = pl.multiple_of(j * tq, tq)
        acc_scr[...] = al_ref[...] * acc_scr[...] + jnp.dot(
            p_ref[...], v_ref[pl.ds(off, tq), :], preferred_element_type=F32)

    def softmax(s_ref, p_ref, al_ref, bias_fn):
        for c in range(r2 // SOFTMAX_ROWS):
            rows = slice(c * SOFTMAX_ROWS, (c + 1) * SOFTMAX_ROWS)
            x = s_ref[rows, :]
            if bias_fn is not None:
                x = x + bias_fn(c)
            m_old = m_scr[rows, :]
            m_new = jnp.maximum(m_old, jnp.max(x, axis=1, keepdims=True))
            alpha = jnp.exp2(m_old - m_new)
            p = jnp.exp2(x - jnp.concatenate([m_new] * n_rep, axis=1))
            p_sum = p[:, 0:LANES]
            for t in range(1, n_rep):
                p_sum = p_sum + p[:, t * LANES:(t + 1) * LANES]
            m_scr[rows, :] = m_new
            l_scr[rows, :] = alpha * l_scr[rows, :] + p_sum
            al_ref[rows, :] = alpha
            p_ref[rows, :] = p.astype(BF16)

    def diag_bias(c):
        r0 = (c * SOFTMAX_ROWS) % tq
        return diag_bias_ref[r0:r0 + SOFTMAX_ROWS, :]

    scores(0, s_x)
    m_scr[...] = jnp.full(m_scr.shape, M_INIT, F32)
    l_scr[...] = jnp.zeros(l_scr.shape, F32)
    acc_scr[...] = jnp.zeros(acc_scr.shape, F32)
    p_y[...] = jnp.zeros(p_y.shape, BF16)
    al_y[...] = jnp.ones(al_y.shape, F32)

    def pair(i, carry):
        e = 2 * i
        weighted_values(jnp.maximum(e - 1, 0), p_y, al_y)
        scores(e + 1, s_y)
        softmax(s_x, p_x, al_x, None)
        weighted_values(e, p_x, al_x)
        scores(e + 2, s_x)
        softmax(s_y, p_y, al_y, None)
        return carry

    lax.fori_loop(0, qi // 2, pair, 0)

    e = 2 * (qi // 2)
    gate = jnp.where(qi % 2 == 1, 0.0, MASKED).astype(F32)
    weighted_values(jnp.maximum(e - 1, 0), p_y, al_y)
    scores(qi, s_y)
    softmax(s_x, p_x, al_x, lambda c: gate)
    weighted_values(e, p_x, al_x)
    softmax(s_y, p_y, al_y, diag_bias)
    weighted_values(qi, p_y, al_y)
    return jnp.sum(l_scr[...], axis=1, keepdims=True), acc_scr[...]


def _fox_kernel(q_ref, k_ref, v_ref, cs_ref, eq_ref, ek_ref, o_ref, diag_ref, kaug_ref, *scratch,
                tq):
    kaug_ref[:, 0:LANES] = k_ref[...]
    kaug_ref[:, LANES:] = jnp.dot(cs_ref[...], ek_ref[0], preferred_element_type=F32).astype(BF16)
    row = lax.broadcasted_iota(jnp.int32, (tq, tq), 0)
    col = lax.broadcasted_iota(jnp.int32, (tq, tq), 1)
    diag_ref[...] = jnp.where(col <= row, 0.0, MASKED)

    def query_tile(qi, carry):
        rows = pl.ds(pl.multiple_of(qi * tq, tq), tq)
        cs_q = cs_ref[rows, :]
        feats = [jnp.dot(cs_q, eq_ref[0, sub], preferred_element_type=F32).astype(BF16)
                 for sub in range(2)]
        qaug = _stack_queries(q_ref[rows, :], feats[0], feats[1])
        l, acc = _flash_pipeline(qaug, kaug_ref, v_ref, scratch, qi, tq, diag_ref)
        out = acc / l
        lane = lax.broadcasted_iota(jnp.int32, (tq, LANES), 1)
        o_ref[rows, :] = jnp.where(lane < HEAD_DIM, out[0:tq], out[tq:2 * tq]).astype(o_ref.dtype)
        return carry

    lax.fori_loop(0, q_ref.shape[0] // tq, query_tile, 0)


def _fox_feature_maps(n_pairs):
    one_row = 3 * LANES
    ek = np.zeros((n_pairs, 4 * LANES, LANES), np.float32)
    eq = np.zeros((n_pairs, 2, 4 * LANES, LANES), np.float32)
    for p in range(n_pairs):
        for sub in range(2):
            head, base = 2 * p + sub, 6 * sub
            for t in range(3):
                ek[p, one_row, base + t] = 1.0
                ek[p, t * LANES + head, base + 3 + t] = -1.0
                eq[p, sub, t * LANES + head, base + t] = 1.0
                eq[p, sub, one_row, base + 3 + t] = 1.0
    return jnp.asarray(eq, BF16), jnp.asarray(ek, BF16)


def _fox_attention(qkv, csplit, *, batch, seq):
    n, width3 = qkv.shape
    n_pairs = width3 // 3 // LANES
    tq = min(ATTN_TILE, seq)
    eq, ek = _fox_feature_maps(n_pairs)
    return pl.pallas_call(
        functools.partial(_fox_kernel, tq=tq),
        out_shape=jax.ShapeDtypeStruct((n, n_pairs * LANES), BF16),
        grid=(batch, n_pairs),
        in_specs=[pl.BlockSpec((seq, LANES), lambda b, h: (b, h)),
                  pl.BlockSpec((seq, LANES), lambda b, h: (b, n_pairs + h)),
                  pl.BlockSpec((seq, LANES), lambda b, h: (b, 2 * n_pairs + h)),
                  pl.BlockSpec((seq, csplit.shape[1]), lambda b, h: (b, 0)),
                  pl.BlockSpec((1,) + eq.shape[1:], lambda b, h: (h, 0, 0, 0)),
                  pl.BlockSpec((1,) + ek.shape[1:], lambda b, h: (h, 0, 0))],
        out_specs=pl.BlockSpec((seq, LANES), lambda b, h: (b, h)),
        scratch_shapes=[pltpu.VMEM((tq, tq), F32)] + _flash_scratch(tq, seq),
        compiler_params=_params(2),
        name="fox_attention",
    )(qkv, qkv, qkv, csplit, eq, ek)


def _diff_kernel(slopes_ref, q_ref, k_ref, v_ref, lam_ref, g_ref, o_ref, diag_ref, kaug_ref,
                 *scratch, tq, lam_init):
    h = pl.program_id(1)
    seq = k_ref.shape[0]
    slope2 = slopes_ref[h] * LOG2E

    def bias_lanes(n_rows, first, base, other, other_value):
        lane = lax.broadcasted_iota(jnp.int32, (n_rows, LANES), 1)
        pos = (lax.broadcasted_iota(jnp.int32, (n_rows, LANES), 0) + first).astype(F32) * slope2
        terms = [t.astype(F32) for t in _split3(pos)]
        feat = jnp.where((lane >= other) & (lane < other + 3), other_value, 0.0)
        for t, term in enumerate(terms):
            feat = jnp.where(lane == base + t, term, feat)
        return feat.astype(BF16)

    kaug_ref[:, 0:LANES] = k_ref[...]
    kaug_ref[:, LANES:] = bias_lanes(seq, 0, 0, 3, -1.0)
    row = lax.broadcasted_iota(jnp.int32, (tq, tq), 0)
    col = lax.broadcasted_iota(jnp.int32, (tq, tq), 1)
    ahead = jnp.maximum(col - row, 0).astype(F32)
    diag_ref[...] = jnp.where((col // CHUNK) <= (row // CHUNK), -2.0 * slope2 * ahead, MASKED)
    t1 = jnp.sum(lam_ref[0:1, :] * lam_ref[1:2, :], axis=-1, keepdims=True)
    t2 = jnp.sum(lam_ref[2:3, :] * lam_ref[3:4, :], axis=-1, keepdims=True)
    lam = jnp.exp(t1) - jnp.exp(t2) + lam_init

    def query_tile(qi, carry):
        rows = pl.ds(pl.multiple_of(qi * tq, tq), tq)
        qfeat = bias_lanes(tq, qi * tq, 3, 0, 1.0)
        qaug = _stack_queries(q_ref[rows, :], qfeat, qfeat)
        l, acc = _flash_pipeline(qaug, kaug_ref, v_ref, scratch, qi, tq, diag_ref)
        out = acc / l
        o = out[0:tq] - lam * out[tq:2 * tq]
        o = o * lax.rsqrt(jnp.mean(o * o, axis=-1, keepdims=True) + RMS_EPS)
        o_ref[rows, :] = (o * g_ref[...] * (1.0 - lam_init)).astype(o_ref.dtype)
        return carry

    lax.fori_loop(0, seq // tq, query_tile, 0)


def _diff_attention(h, slopes, lam_rows, subln_g, *, batch, seq, lam_init):
    n, width3 = h.shape
    n_heads = width3 // 3 // LANES
    tq = min(ATTN_TILE, seq)
    grid_spec = pltpu.PrefetchScalarGridSpec(
        num_scalar_prefetch=1,
        grid=(batch, n_heads),
        in_specs=[pl.BlockSpec((seq, LANES), lambda b, hd, s: (b, hd)),
                  pl.BlockSpec((seq, LANES), lambda b, hd, s: (b, n_heads + hd)),
                  pl.BlockSpec((seq, LANES), lambda b, hd, s: (b, 2 * n_heads + hd)),
                  pl.BlockSpec(lam_rows.shape, lambda b, hd, s: (0, 0)),
                  pl.BlockSpec(subln_g.shape, lambda b, hd, s: (0, 0))],
        out_specs=pl.BlockSpec((seq, LANES), lambda b, hd, s: (b, hd)),
        scratch_shapes=[pltpu.VMEM((tq, tq), F32)] + _flash_scratch(tq, seq),
    )
    return pl.pallas_call(
        functools.partial(_diff_kernel, tq=tq, lam_init=lam_init),
        out_shape=jax.ShapeDtypeStruct((n, n_heads * LANES), BF16),
        grid_spec=grid_spec,
        compiler_params=_params(2),
        name="diff_attention",
    )(slopes, h, h, h, lam_rows, subln_g)


def _route(lg):
    lane = lax.broadcasted_iota(jnp.int32, lg.shape, 1).astype(F32)
    low = jnp.float32(-3e38)
    is_g = lane < N_GROUPS
    gmax = jnp.max(jnp.where(is_g, lg, low), axis=-1, keepdims=True)
    gsel = jnp.min(jnp.where(is_g & (lg == gmax), lane, float(LANES)), axis=-1, keepdims=True)
    gsum = jnp.sum(jnp.where(is_g, jnp.exp(lg - gmax), 0.0), axis=-1, keepdims=True)
    gw = 1.0 / gsum
    lo = N_GROUPS + EXPERTS_PER_GROUP * gsel
    in_grp = (lane >= lo) & (lane < lo + EXPERTS_PER_GROUP)
    el = jnp.where(in_grp, lg, low)
    v1 = jnp.max(el, axis=-1, keepdims=True)
    i1 = jnp.min(jnp.where(in_grp & (lg == v1), lane, float(LANES)), axis=-1, keepdims=True)
    rest = in_grp & (lane != i1)
    v2 = jnp.max(jnp.where(rest, lg, low), axis=-1, keepdims=True)
    i2 = jnp.min(jnp.where(rest & (lg == v2), lane, float(LANES)), axis=-1, keepdims=True)
    t = jnp.exp(v2 - v1)
    w1 = gw / (1.0 + t)
    w2 = gw * (t / (1.0 + t))
    rec = jnp.where(lane == 0.0, i1 - N_GROUPS,
                    jnp.where(lane == 1.0, i2 - N_GROUPS,
                              jnp.where(lane == 2.0, w1, jnp.where(lane == 3.0, w2, 0.0))))
    return rec


def _mix_ln_route(mix, x_ref, lng_ref, lnb_ref, wr_ref, br_ref, xo_ref, xs_ref, rec_ref, alpha):
    xo = _layer_norm(alpha * x_ref[...] + mix, lng_ref[...], lnb_ref[...])
    xo_ref[...] = xo
    tm = xo.shape[0]
    for j in range(TOKEN_SLABS):
        xs_ref[pl.ds(j, tm, stride=TOKEN_SLABS), :] = xo[:, j * LANES:(j + 1) * LANES]
    d = xo.shape[1]
    x_hi = xo.astype(BF16)
    x_lo = (xo - x_hi.astype(F32)).astype(BF16)
    lg = (jnp.dot(x_hi, wr_ref[0:d, :], preferred_element_type=F32)
          + jnp.dot(x_lo, wr_ref[0:d, :], preferred_element_type=F32)
          + jnp.dot(x_hi, wr_ref[d:2 * d, :], preferred_element_type=F32)) + br_ref[...]
    rec_ref[...] = _route(lg)


def _outproj_even_kernel(gates_ref, attn_ref, x_ref, wout_ref, convw_ref, lng_ref, lnb_ref,
                         wr_ref, br_ref, xo_ref, xs_ref, rec_ref, tail_ref, *, tiles_per_seq,
                         alpha):
    i = pl.program_id(0)
    tm = x_ref.shape[0]
    cw = gates_ref.shape[1] // 3
    gb = gates_ref[:, 0:cw]
    u = gates_ref[:, cw:2 * cw] * gates_ref[:, 2 * cw:3 * cw]

    @pl.when(i % tiles_per_seq == 0)
    def _():
        tail_ref[...] = jnp.zeros_like(tail_ref)

    tail = tail_ref[...]
    row = lax.broadcasted_iota(jnp.int32, (tm, 1), 0)
    u1 = jnp.where(row == 0, tail[7:8, :], pltpu.roll(u, 1, axis=0))
    u2 = jnp.where(row == 0, tail[6:7, :],
                   jnp.where(row == 1, tail[7:8, :], pltpu.roll(u, 2, axis=0)))
    tail_ref[...] = u[tm - 8:tm, :]
    y = u2 * convw_ref[0:1, :] + u1 * convw_ref[1:2, :] + u * convw_ref[2:3, :]
    a = (gb * y).astype(BF16)
    mix = (jnp.dot(a, wout_ref[0:cw, :], preferred_element_type=F32)
           + jnp.dot(attn_ref[...], wout_ref[cw:, :], preferred_element_type=F32))
    _mix_ln_route(mix, x_ref, lng_ref, lnb_ref, wr_ref, br_ref, xo_ref, xs_ref, rec_ref, alpha)


def _outproj_odd_kernel(attn_ref, x_ref, wout_ref, lng_ref, lnb_ref, wr_ref, br_ref,
                        xo_ref, xs_ref, rec_ref, *, alpha):
    mix = jnp.dot(attn_ref[...], wout_ref[...], preferred_element_type=F32)
    _mix_ln_route(mix, x_ref, lng_ref, lnb_ref, wr_ref, br_ref, xo_ref, xs_ref, rec_ref, alpha)


def _outproj(acts, xf, wout, convw, lng, lnb, wr, br, *, seq, alpha):
    n, d = xf.shape
    tm = min(ROW_TILE, seq)
    row_spec = lambda width: pl.BlockSpec((tm, width), lambda i: (i, 0))
    full = lambda a: pl.BlockSpec(a.shape, lambda i: (0, 0))
    even = convw is not None
    if even:
        body = functools.partial(_outproj_even_kernel, tiles_per_seq=seq // tm, alpha=alpha)
        operands = (*acts, xf, wout, convw, lng, lnb, wr, br)
        scratch = [pltpu.VMEM((8, acts[0].shape[1] // 3), F32)]
    else:
        body = functools.partial(_outproj_odd_kernel, alpha=alpha)
        operands = (*acts, xf, wout, lng, lnb, wr, br)
        scratch = []
    in_specs = [row_spec(a.shape[1]) for a in acts] + [row_spec(d)]
    in_specs += [full(a) for a in operands[len(acts) + 1:]]
    return pl.pallas_call(
        body,
        out_shape=(jax.ShapeDtypeStruct((n, d), F32),
                   jax.ShapeDtypeStruct((n * TOKEN_SLABS, LANES), F32),
                   jax.ShapeDtypeStruct((n, LANES), F32)),
        grid=(n // tm,),
        in_specs=in_specs,
        out_specs=(row_spec(d), pl.BlockSpec((tm * TOKEN_SLABS, LANES), lambda i: (i, 0)),
                   row_spec(LANES)),
        scratch_shapes=scratch,
        compiler_params=_params(1),
        name="outproj_even" if even else "outproj_odd",
    )(*operands)


def _plan_kernel(rec_ref, start_ref, pos_ref, run_ref):
    i = pl.program_id(0)

    @pl.when(i == 0)
    def _():
        run_ref[...] = jnp.zeros_like(run_ref)

    rec = rec_ref[...]
    tm = rec.shape[0]
    lane = lax.broadcasted_iota(jnp.int32, rec.shape, 1).astype(F32)
    oh1 = lane == rec[:, 0:1]
    oh2 = lane == rec[:, 1:2]
    oh = jnp.where(oh1 | oh2, 1.0, 0.0)
    r = lax.broadcasted_iota(jnp.int32, (tm, tm), 0)
    c = lax.broadcasted_iota(jnp.int32, (tm, tm), 1)
    earlier = jnp.where(c < r, 1.0, 0.0).astype(BF16)
    base = (jnp.dot(earlier, oh.astype(BF16), preferred_element_type=F32)
            + run_ref[0:1, :] + start_ref[...])
    p1 = jnp.sum(jnp.where(oh1, base, 0.0), axis=1, keepdims=True)
    p2 = jnp.sum(jnp.where(oh2, base, 0.0), axis=1, keepdims=True)
    run_ref[...] = run_ref[...] + jnp.sum(oh, axis=0, keepdims=True)
    pos_ref[...] = jnp.where(lane == 0.0, p1, jnp.where(lane == 1.0, p2, 0.0)).astype(jnp.int32)


def _plan_rows(rec, seg_start):
    n = rec.shape[0]
    tm = ROW_TILE
    return pl.pallas_call(
        _plan_kernel,
        out_shape=jax.ShapeDtypeStruct((n, LANES), jnp.int32),
        grid=(n // tm,),
        in_specs=[pl.BlockSpec((tm, LANES), lambda i: (i, 0)),
                  pl.BlockSpec(seg_start.shape, lambda i: (0, 0))],
        out_specs=pl.BlockSpec((tm, LANES), lambda i: (i, 0)),
        scratch_shapes=[pltpu.VMEM((8, LANES), F32)],
        compiler_params=_params(1),
        name="moe_plan",
    )(rec, seg_start)


def _dispatch_plan(rec, n_tok, rows):
    eid = rec[:, 0:2].astype(jnp.int32).reshape(-1)
    counts = jnp.sum((eid[:, None] == jnp.arange(N_EXPERTS, dtype=jnp.int32)[None, :])
                     .astype(jnp.int32), axis=0)
    padded = (counts + rows - 1) // rows * rows
    pad_end = jnp.cumsum(padded)
    seg_start = jnp.pad((pad_end - padded).astype(F32), (0, LANES - N_EXPERTS))[None, :]
    pos = _plan_rows(rec, seg_start)[:, 0:2].reshape(-1)
    cap = 2 * n_tok + N_EXPERTS * rows
    n_blocks = cap // rows
    n_pairs = 2 * n_tok
    row_pair = jnp.full((cap,), -1, jnp.int32).at[pos].set(jnp.arange(n_pairs, dtype=jnp.int32))
    src = jnp.where(row_pair >= 0, (row_pair >> 1) * TOKEN_SLABS, 0)
    look_ahead = GATHER_BUFFERS - 1
    src_ext = jnp.concatenate([src, jnp.zeros((look_ahead * rows,), jnp.int32)])
    src_ext = src_ext.reshape(n_blocks + look_ahead, 1, rows)
    tm = ROW_TILE
    n_tiles = n_tok // tm
    pos_ext = jnp.transpose((pos * TOKEN_SLABS).reshape(n_tiles, tm, 2), (0, 2, 1))
    pos_ext = pos_ext.reshape(n_tiles, 2 * tm)
    pos_ext = jnp.concatenate([pos_ext, jnp.zeros((look_ahead, 2 * tm), jnp.int32)])
    pos_ext = pos_ext.reshape(n_tiles + look_ahead, 1, 2 * tm)
    blk_start = jnp.arange(n_blocks, dtype=jnp.int32) * rows
    blk_e = jnp.minimum(jnp.sum((pad_end[None, :] <= blk_start[:, None]).astype(jnp.int32), axis=1),
                        N_EXPERTS - 1)
    return src_ext, pos_ext, blk_e


def _start_rows(src_hbm, idx_ref, idx_base, dst_ref, slot, sem, n, priority=None):
    n_sems = sem.shape[1]
    for r in range(n):
        first = pl.multiple_of(idx_ref[0, 0, idx_base + r], TOKEN_SLABS)
        pltpu.make_async_copy(src_hbm.at[pl.ds(first, TOKEN_SLABS)],
                              dst_ref.at[slot, pl.ds(r * TOKEN_SLABS, TOKEN_SLABS)],
                              sem.at[slot, r % n_sems]
                              ).start(priority=r % 2 if priority is None else priority)


def _wait_rows(src_hbm, dst_ref, slot, sem):
    n_sems = sem.shape[1]
    share = dst_ref.shape[1] // n_sems
    for k in range(n_sems):
        pltpu.make_async_copy(src_hbm.at[pl.ds(0, share)], dst_ref.at[slot, pl.ds(0, share)],
                              sem.at[slot, k]).wait()


def _expert_mlp_kernel(blk_e_ref, src0_ref, src1_ref, src2_ref, x_hbm, wg_ref, wu_ref, wd_ref,
                       out_ref, xbuf, wgu_s, wd_s, sem):
    i = pl.program_id(0)
    rows = xbuf.shape[1] // TOKEN_SLABS
    dh = wg_ref.shape[3]
    slot = i % GATHER_BUFFERS
    ahead1 = (i + 1) % GATHER_BUFFERS
    ahead2 = (i + 2) % GATHER_BUFFERS

    @pl.when(i == 0)
    def _():
        _start_rows(x_hbm, src0_ref, 0, xbuf, 0, sem, rows, GATHER_QUEUE)
        _start_rows(x_hbm, src1_ref, 0, xbuf, 1, sem, rows, GATHER_QUEUE)

    _wait_rows(x_hbm, xbuf, slot, sem)
    changed = jnp.logical_or(i == 0, blk_e_ref[i] != blk_e_ref[jnp.maximum(i - 1, 0)])

    @pl.when(changed)
    def _():
        wgu_s[:, 0:dh] = wg_ref[0, 0].astype(BF16)
        wgu_s[:, dh:2 * dh] = wu_ref[0, 0].astype(BF16)
        wd_s[...] = wd_ref[0, 0].astype(BF16)

    x = jnp.concatenate([xbuf[slot, pl.ds(j, rows, stride=TOKEN_SLABS), :].astype(BF16)
                         for j in range(TOKEN_SLABS)], axis=1)
    _start_rows(x_hbm, src2_ref, 0, xbuf, ahead2, sem, rows, GATHER_QUEUE)
    gu = jnp.dot(x, wgu_s[...], preferred_element_type=F32)
    g = gu[:, 0:dh]
    hid = (g / (1.0 + jnp.exp(-g))) * gu[:, dh:2 * dh]
    y = jnp.dot(hid.astype(BF16), wd_s[...], preferred_element_type=F32)
    for j in range(TOKEN_SLABS):
        out_ref[pl.ds(j, rows, stride=TOKEN_SLABS), :] = y[:, j * LANES:(j + 1) * LANES]

    @pl.when(i == pl.num_programs(0) - 1)
    def _():
        _wait_rows(x_hbm, xbuf, ahead1, sem)
        _wait_rows(x_hbm, xbuf, ahead2, sem)


def _expert_mlp(x_slabs, w_gate, w_up, w_down, layer, src_ext, blk_e, rows):
    d = w_gate.shape[2]
    dh = w_gate.shape[3]
    n_blocks = blk_e.shape[0]
    idx_spec = lambda shift: pl.BlockSpec((1, 1, rows), lambda i, be: (i + shift, 0, 0),
                                          memory_space=pltpu.SMEM)
    grid_spec = pltpu.PrefetchScalarGridSpec(
        num_scalar_prefetch=1,
        grid=(n_blocks,),
        in_specs=[idx_spec(0), idx_spec(1), idx_spec(2),
                  pl.BlockSpec(memory_space=pl.ANY),
                  pl.BlockSpec((1, 1, d, dh), lambda i, be: (layer, be[i], 0, 0)),
                  pl.BlockSpec((1, 1, d, dh), lambda i, be: (layer, be[i], 0, 0)),
                  pl.BlockSpec((1, 1, dh, d), lambda i, be: (layer, be[i], 0, 0))],
        out_specs=pl.BlockSpec((rows * TOKEN_SLABS, LANES), lambda i, be: (i, 0)),
        scratch_shapes=[pltpu.VMEM((GATHER_BUFFERS, rows * TOKEN_SLABS, LANES), F32),
                        pltpu.VMEM((d, 2 * dh), BF16), pltpu.VMEM((dh, d), BF16),
                        pltpu.SemaphoreType.DMA((GATHER_BUFFERS, ROW_SEMS))],
    )
    return pl.pallas_call(
        _expert_mlp_kernel,
        out_shape=jax.ShapeDtypeStruct((n_blocks * rows * TOKEN_SLABS, LANES), F32),
        grid_spec=grid_spec,
        compiler_params=_params(1),
        name="moe_experts",
    )(blk_e, src_ext, src_ext, src_ext, x_slabs, w_gate, w_up, w_down)


def _combine_ln_kernel(pos0_ref, pos1_ref, pos2_ref, yb_hbm, rec_ref, x_ref, lng_ref, lnb_ref,
                       *refs, alpha, next_mixer, tiles_per_seq):
    n_w = {"even": 3, "odd": 1, None: 0}[next_mixer]
    n_out = {"even": 3, "odd": 1, None: 0}[next_mixer]
    w_refs = refs[:n_w]
    xo_ref = refs[n_w]
    proj_refs = refs[n_w + 1:n_w + 1 + n_out]
    ya_buf, yb_buf, sem_a, sem_b = refs[n_w + 1 + n_out:n_w + 5 + n_out]
    i = pl.program_id(0)
    tm = x_ref.shape[0]
    slot = i % GATHER_BUFFERS
    ahead1 = (i + 1) % GATHER_BUFFERS
    ahead2 = (i + 2) % GATHER_BUFFERS

    @pl.when(i == 0)
    def _():
        _start_rows(yb_hbm, pos0_ref, 0, ya_buf, 0, sem_a, tm)
        _start_rows(yb_hbm, pos0_ref, tm, yb_buf, 0, sem_b, tm)
        _start_rows(yb_hbm, pos1_ref, 0, ya_buf, 1, sem_a, tm)
        _start_rows(yb_hbm, pos1_ref, tm, yb_buf, 1, sem_b, tm)

    _wait_rows(yb_hbm, ya_buf, slot, sem_a)
    _wait_rows(yb_hbm, yb_buf, slot, sem_b)

    def token_rows(buf):
        return jnp.concatenate([buf[slot, pl.ds(j, tm, stride=TOKEN_SLABS), :]
                                for j in range(TOKEN_SLABS)], axis=1)

    z = alpha * x_ref[...] + (rec_ref[:, 2:3] * token_rows(ya_buf)
                              + rec_ref[:, 3:4] * token_rows(yb_buf))
    _start_rows(yb_hbm, pos2_ref, 0, ya_buf, ahead2, sem_a, tm)
    _start_rows(yb_hbm, pos2_ref, tm, yb_buf, ahead2, sem_b, tm)
    xo = _layer_norm(z, lng_ref[...], lnb_ref[...])
    xo_ref[...] = xo
    if next_mixer == "even":
        _project_even(xo.astype(BF16), *w_refs, *proj_refs, refs[-1], tiles_per_seq)
    elif next_mixer == "odd":
        _project_odd(xo.astype(BF16), *w_refs, *proj_refs)

    @pl.when(i == pl.num_programs(0) - 1)
    def _():
        for ahead in (ahead1, ahead2):
            _wait_rows(yb_hbm, ya_buf, ahead, sem_a)
            _wait_rows(yb_hbm, yb_buf, ahead, sem_b)


def _combine_ln(yb, pos_ext, rec, xf, lng, lnb, next_mixer, next_weights, *, seq, alpha):
    n, d = xf.shape
    tm = min(ROW_TILE, seq)
    idx_spec = lambda shift: pl.BlockSpec((1, 1, 2 * tm), lambda i: (i + shift, 0, 0),
                                          memory_space=pltpu.SMEM)
    outs = [(n, d, F32)]
    scratch = [pltpu.VMEM((GATHER_BUFFERS, tm * TOKEN_SLABS, LANES), F32),
               pltpu.VMEM((GATHER_BUFFERS, tm * TOKEN_SLABS, LANES), F32),
               pltpu.SemaphoreType.DMA((GATHER_BUFFERS, 1)),
               pltpu.SemaphoreType.DMA((GATHER_BUFFERS, 1))]
    if next_mixer == "even":
        outs += _even_projection_shapes(n, d, next_weights[0])
        scratch.append(pltpu.VMEM((8, LANES), F32))
    elif next_mixer == "odd":
        outs.append((n, next_weights[0].shape[1], BF16))
    return pl.pallas_call(
        functools.partial(_combine_ln_kernel, alpha=alpha, next_mixer=next_mixer,
                          tiles_per_seq=seq // tm),
        out_shape=tuple(jax.ShapeDtypeStruct((r, c), t) for r, c, t in outs),
        grid=(n // tm,),
        in_specs=[idx_spec(0), idx_spec(1), idx_spec(2),
                  pl.BlockSpec(memory_space=pl.ANY),
                  pl.BlockSpec((tm, LANES), lambda i: (i, 0)),
                  pl.BlockSpec((tm, d), lambda i: (i, 0)),
                  pl.BlockSpec(lng.shape, lambda i: (0, 0)),
                  pl.BlockSpec(lnb.shape, lambda i: (0, 0))]
                 + [pl.BlockSpec(w.shape, lambda i: (0, 0)) for w in next_weights],
        out_specs=tuple(pl.BlockSpec((tm, c), lambda i: (i, 0)) for _, c, _ in outs),
        scratch_shapes=scratch,
        compiler_params=_params(1),
        name="moe_combine_ln",
    )(pos_ext, pos_ext, pos_ext, yb, rec, xf, lng, lnb, *next_weights)


def kernel(x, ab_w_in, ab_b_forget, ab_conv_w, ab_w_out, c_w_in, c_lam_q1, c_lam_k1, c_lam_q2,
           c_lam_k2, c_subln_g, c_w_out, ln_mix_g, ln_mix_b, ln_ffn_g, ln_ffn_b, moe_w_group,
           moe_b_group, moe_w_expert, moe_b_expert, moe_w_gate, moe_w_up, moe_w_down):
    batch, seq, d = x.shape
    depth = ln_mix_g.shape[0]
    n_tok = batch * seq
    alpha = (2.0 * depth) ** 0.25
    q_scale = HEAD_DIM ** -0.5 * LOG2E
    conv_ch = ab_conv_w.shape[1]
    fox_heads = ab_b_forget.shape[1]
    n_diff_heads = d // (2 * HEAD_DIM)
    slopes = jnp.asarray(2.0 ** (-8.0 * np.arange(1, n_diff_heads + 1) / n_diff_heads), F32)
    xf = x.reshape(n_tok, d)

    def pad_lanes(v):
        return jnp.pad(v, (0, LANES - v.shape[0]))

    def projection_weights(layer):
        i = layer // 2
        if layer % 2 == 0:
            n_main = ab_w_in.shape[2] - fox_heads
            q_lo = 3 * conv_ch
            col_scale = np.ones((n_main,), np.float32)
            col_scale[q_lo:q_lo + fox_heads * HEAD_DIM] = q_scale
            w_main = (ab_w_in[i, :, :n_main] * col_scale).astype(BF16)
            wf = jnp.pad(ab_w_in[i, :, n_main:], ((0, 0), (0, LANES - fox_heads))).astype(BF16)
            return "even", (w_main, wf, pad_lanes(ab_b_forget[i])[None, :])
        col_scale = np.ones((c_w_in.shape[2],), np.float32)
        col_scale[0:d] = q_scale
        return "odd", ((c_w_in[i] * col_scale).astype(BF16),)

    projected = _inproj_even(xf, *projection_weights(0)[1], seq=seq)
    for layer in range(depth):
        i = layer // 2
        w_router = jnp.pad(jnp.concatenate([moe_w_group[layer], moe_w_expert[layer]], axis=1),
                           ((0, 0), (0, LANES - N_GROUPS - N_EXPERTS)))
        w_router_hi = w_router.astype(BF16)
        w_router = jnp.concatenate(
            [w_router_hi, (w_router - w_router_hi.astype(F32)).astype(BF16)], axis=0)
        b_router = pad_lanes(jnp.concatenate([moe_b_group[layer], moe_b_expert[layer]]))[None, :]
        lng, lnb = ln_mix_g[layer][None, :], ln_mix_b[layer][None, :]
        if layer % 2 == 0:
            gates, qkv, csplit = projected
            attn = _fox_attention(qkv, csplit, batch=batch, seq=seq)
            convw = jnp.pad(jnp.transpose(ab_conv_w[i]), ((0, 8 - ab_conv_w.shape[2]), (0, 0)))
            xf, x_slabs, rec = _outproj((gates, attn), xf, ab_w_out[i].astype(BF16), convw, lng,
                                        lnb, w_router, b_router, seq=seq, alpha=alpha)
        else:
            (h,) = projected
            lam_init = 0.8 - 0.6 * math.exp(-0.3 * layer)
            lam_rows = jnp.pad(jnp.stack([pad_lanes(c_lam_q1[i]), pad_lanes(c_lam_k1[i]),
                                          pad_lanes(c_lam_q2[i]), pad_lanes(c_lam_k2[i])]),
                               ((0, 4), (0, 0)))
            attn = _diff_attention(h, slopes, lam_rows, c_subln_g[i][None, :],
                                   batch=batch, seq=seq, lam_init=lam_init)
            xf, x_slabs, rec = _outproj((attn,), xf, c_w_out[i].astype(BF16), None, lng, lnb,
                                        w_router, b_router, seq=seq, alpha=alpha)

        src_ext, pos_ext, blk_e = _dispatch_plan(rec, n_tok, MOE_ROWS)
        yb = _expert_mlp(x_slabs, moe_w_gate, moe_w_up, moe_w_down, layer, src_ext, blk_e,
                         MOE_ROWS)
        next_mixer, next_weights = projection_weights(layer + 1) if layer + 1 < depth else (None, ())
        xf, *projected = _combine_ln(yb, pos_ext, rec, xf, ln_ffn_g[layer][None, :],
                                     ln_ffn_b[layer][None, :], next_mixer, next_weights,
                                     seq=seq, alpha=alpha)
    return xf.reshape(batch, seq, d)
```
